```python
import jax, jax.numpy as jnp
from jax import lax
import numpy as np

D_MODEL = 2048
BATCH = 4
SEQ = 2048
DEPTH = 4

N_EVEN = (DEPTH + 1) // 2
N_ODD = DEPTH // 2

A_KEY = 128
A_VAL = 128
A_HEADS = D_MODEL // A_VAL
A_KDIM = A_HEADS * A_KEY
A_WIDTH = A_HEADS * A_VAL
A_CHUNK = 64
A_F_MIN = 1e-6
B_HEAD_DIM = 64
B_WIDTH = D_MODEL
B_HEADS = B_WIDTH // B_HEAD_DIM
B_GROUPS = 4
B_STATE = 128
B_CONV = 4
B_CHUNK = 128
B_CONV_CH = B_WIDTH + 2 * B_GROUPS * B_STATE
EVEN_IN_SIZES = (A_KDIM, A_KDIM, A_WIDTH, A_WIDTH, B_WIDTH, B_CONV_CH, B_HEADS)
EVEN_IN = sum(EVEN_IN_SIZES)
EVEN_MIX = A_WIDTH + B_WIDTH
C_WIDTH = D_MODEL
C_KERNEL = 31
FFN_HIDDEN = -(-8 * D_MODEL // (3 * 256)) * 256
RMS_EPS = 1e-6
LN_EPS = 1e-5

kernel_name = 'hgrn2_mamba2_conformer_hybrid'


def rms_norm(x, g, eps=RMS_EPS):
    x32 = x.astype(jnp.float32)
    y = x32 * lax.rsqrt(jnp.mean(x32 * x32, axis=-1, keepdims=True) + eps)
    return (y * g.astype(jnp.float32)).astype(x.dtype)


def group_rms_norm(x, g, n_groups, eps=RMS_EPS):
    shp = x.shape
    x32 = x.astype(jnp.float32).reshape(shp[:-1] + (n_groups, shp[-1] // n_groups))
    y = x32 * lax.rsqrt(jnp.mean(x32 * x32, axis=-1, keepdims=True) + eps)
    return y.reshape(shp) * g.astype(jnp.float32)


def layer_norm(x, g, b, eps=LN_EPS):
    x32 = x.astype(jnp.float32)
    mu = jnp.mean(x32, axis=-1, keepdims=True)
    xc = x32 - mu
    y = xc * lax.rsqrt(jnp.mean(xc * xc, axis=-1, keepdims=True) + eps)
    return (y * g.astype(jnp.float32) + b.astype(jnp.float32)).astype(x.dtype)


def causal_depthwise_conv(x, w):
    k, c = w.shape
    return lax.conv_general_dilated(
        x, w[:, None, :].astype(x.dtype), window_strides=(1,), padding=[(k - 1, 0)],
        dimension_numbers=('NWC', 'WIO', 'NWC'), feature_group_count=c)


def masked_exp(mask, t):
    return jnp.where(mask, jnp.exp(jnp.where(mask, t, 0.0)), 0.0)


def hgrn2_chunkwise(q, k, v, log_f):
    bsz, s, h, dk = q.shape
    dv = v.shape[-1]
    nc = s // A_CHUNK

    def to_chunks(t):
        return t.reshape(bsz, nc, A_CHUNK, h, t.shape[-1]).transpose(1, 0, 3, 2, 4)

    causal = jnp.tril(jnp.ones((A_CHUNK, A_CHUNK), dtype=bool))[:, :, None]

    def step(state, inp):
        qc, kc, vc, gc = inp
        b = jnp.cumsum(gc, axis=2)
        o_inter = jnp.einsum('bhtk,bhkv->bhtv', qc * jnp.exp(b), state)
        diff = b[:, :, :, None, :] - b[:, :, None, :, :]
        decay = masked_exp(causal, diff)
        scores = jnp.einsum('bhtk,bhsk,bhtsk->bhts', qc, kc, decay)
        o_intra = jnp.einsum('bhts,bhsv->bhtv', scores, vc)
        b_last = b[:, :, -1, :]
        state = state * jnp.exp(b_last)[..., None] + jnp.einsum(
            'bhsk,bhsv->bhkv', kc * jnp.exp(b_last[:, :, None, :] - b), vc)
        return state, o_inter + o_intra

    init = jnp.zeros((bsz, h, dk, dv), jnp.float32)
    _, o = lax.scan(step, init, (to_chunks(q), to_chunks(k), to_chunks(v), to_chunks(log_f)))
    return o.transpose(1, 0, 3, 2, 4).reshape(bsz, s, h * dv)


def ssd_chunked(x, dt, a, bm, cm):
    bsz, s, nh, p = x.shape
    g, n = bm.shape[-2:]
    r = nh // g
    nc = s // B_CHUNK
    L = B_CHUNK
    xdt = (x * dt[..., None]).reshape(bsz, nc, L, g, r, p)
    cs = jnp.cumsum((dt * a).reshape(bsz, nc, L, g, r), axis=2)
    bc = bm.reshape(bsz, nc, L, g, n)
    cc = cm.reshape(bsz, nc, L, g, n)
    causal = jnp.tril(jnp.ones((L, L), dtype=bool))[:, :, None, None]
    seg = cs[:, :, :, None] - cs[:, :, None, :]
    decay = masked_exp(causal, seg)
    cb = jnp.einsum('bctgn,bcsgn->bctsg', cc, bc)
    y_diag = jnp.einsum('bctsg,bctsgr,bcsgrp->bctgrp', cb, decay, xdt)
    decay_to_end = jnp.exp(cs[:, :, -1:] - cs)
    states = jnp.einsum('bcsgn,bcsgr,bcsgrp->bcgrpn', bc, decay_to_end, xdt)
    chunk_decay = jnp.exp(cs[:, :, -1])

    def pass_state(hs, inp):
        st, dec = inp
        return hs * dec[..., None, None] + st, hs

    init = jnp.zeros((bsz, g, r, p, n), jnp.float32)
    _, h_prev = lax.scan(pass_state, init,
                         (states.transpose(1, 0, 2, 3, 4, 5), chunk_decay.transpose(1, 0, 2, 3)))
    h_prev = h_prev.transpose(1, 0, 2, 3, 4, 5)
    y_off = jnp.einsum('bctgn,bcgrpn,bctgr->bctgrp', cc, h_prev, jnp.exp(cs))
    return (y_diag + y_off).reshape(bsz, s, nh, p)


def hgrn2_mamba2_mixer(u, w_in, lower_bound, a_norm, conv_w, conv_b, dt_bias, a_log,
                       d_skip, b_norm, w_out):
    f32 = jnp.float32
    bsz, s, _ = u.shape
    split_points = np.cumsum(EVEN_IN_SIZES)[:-1].tolist()
    q, f_pre, v, gate, z, xbc, dt_raw = jnp.split(u @ w_in, split_points, axis=-1)

    lb = lower_bound.astype(f32)
    sig = jax.nn.sigmoid(f_pre.astype(f32))
    f = lb + (1.0 - lb) * sig
    k = (1.0 - lb) * (1.0 - sig)
    log_f = jnp.log(jnp.maximum(f, A_F_MIN))
    heads = lambda t: t.reshape(bsz, s, A_HEADS, -1)
    o_a = hgrn2_chunkwise(heads(jax.nn.silu(q.astype(f32))), heads(k),
                          heads(v.astype(f32)), heads(log_f))
    o_a = group_rms_norm(o_a, a_norm, A_HEADS) * jax.nn.silu(gate.astype(f32))

    xbc = jax.nn.silu(causal_depthwise_conv(xbc, conv_w) + conv_b.astype(xbc.dtype))
    xs, bm, cm = jnp.split(xbc, [B_WIDTH, B_WIDTH + B_GROUPS * B_STATE], axis=-1)
    dt = jax.nn.softplus(dt_raw.astype(f32) + dt_bias.astype(f32))
    a = -jnp.exp(a_log.astype(f32))
    xh = xs.astype(f32).reshape(bsz, s, B_HEADS, B_HEAD_DIM)
    y = ssd_chunked(xh, dt, a,
                    bm.astype(f32).reshape(bsz, s, B_GROUPS, B_STATE),
                    cm.astype(f32).reshape(bsz, s, B_GROUPS, B_STATE))
    y = (y + d_skip.astype(f32)[:, None] * xh).reshape(bsz, s, B_WIDTH)
    o_b = group_rms_norm(y * jax.nn.silu(z.astype(f32)), b_norm, B_GROUPS)

    mixed = jnp.concatenate([o_a, o_b], axis=-1).astype(u.dtype)
    return mixed @ w_out


def conformer_conv_module(u, w1, b1, dw_w, dw_b, ln_g, ln_b, w2, b2):
    a, g = jnp.split(u @ w1 + b1, 2, axis=-1)
    c = a * jax.nn.sigmoid(g)
    c = causal_depthwise_conv(c, dw_w) + dw_b.astype(c.dtype)
    c = jax.nn.silu(layer_norm(c, ln_g, ln_b))
    return c.astype(u.dtype) @ w2 + b2


def swiglu(u, w_gate, w_up, w_down):
    return (jax.nn.silu(u @ w_gate) * (u @ w_up)) @ w_down


def setup_inputs(seed: int = 0) -> dict:
    key = jax.random.key(seed)
    ks = jax.random.split(key, 32)
    f32 = jnp.float32
    nrm = lambda k, shape, scale: jax.random.normal(k, shape, f32) * scale
    gain = lambda k, shape: 1.0 + 0.05 * jax.random.normal(k, shape, f32)
    dt0 = jnp.exp(jax.random.uniform(ks[10], (N_EVEN, B_HEADS), f32,
                                     minval=float(np.log(1e-3)), maxval=float(np.log(1e-1))))
    return {
        'x': jax.random.normal(ks[0], (BATCH, SEQ, D_MODEL), f32),
        'mix_pre_g': gain(ks[1], (DEPTH, D_MODEL)),
        'mix_post_g': gain(ks[2], (DEPTH, D_MODEL)),
        'ffn_pre_g': gain(ks[3], (DEPTH, D_MODEL)),
        'ffn_post_g': gain(ks[4], (DEPTH, D_MODEL)),
        'hgrn_lb_logits': nrm(ks[5], (N_EVEN, A_KDIM), 0.5),
        'even_w_in': nrm(ks[6], (N_EVEN, D_MODEL, EVEN_IN), D_MODEL ** -0.5),
        'hgrn_norm_g': gain(ks[7], (N_EVEN, A_WIDTH)),
        'ssd_conv_w': nrm(ks[8], (N_EVEN, B_CONV, B_CONV_CH), B_CONV ** -0.5),
        'ssd_conv_b': nrm(ks[9], (N_EVEN, B_CONV_CH), 0.02),
        'ssd_dt_bias': dt0 + jnp.log(-jnp.expm1(-dt0)),
        'ssd_a_log': jnp.log(jax.random.uniform(ks[11], (N_EVEN, B_HEADS), f32, minval=1.0, maxval=16.0)),
        'ssd_d': 1.0 + 0.1 * jax.random.normal(ks[12], (N_EVEN, B_HEADS), f32),
        'ssd_norm_g': gain(ks[13], (N_EVEN, B_WIDTH)),
        'even_w_out': nrm(ks[14], (N_EVEN, EVEN_MIX, D_MODEL), EVEN_MIX ** -0.5),
        'conf_w1': nrm(ks[15], (N_ODD, D_MODEL, 2 * C_WIDTH), D_MODEL ** -0.5),
        'conf_b1': nrm(ks[16], (N_ODD, 2 * C_WIDTH), 0.02),
        'conf_dw_w': nrm(ks[17], (N_ODD, C_KERNEL, C_WIDTH), C_KERNEL ** -0.5),
        'conf_dw_b': nrm(ks[18], (N_ODD, C_WIDTH), 0.02),
        'conf_ln_g': gain(ks[19], (N_ODD, C_WIDTH)),
        'conf_ln_b': nrm(ks[20], (N_ODD, C_WIDTH), 0.02),
        'conf_w2': nrm(ks[21], (N_ODD, C_WIDTH, D_MODEL), C_WIDTH ** -0.5),
        'conf_b2': nrm(ks[22], (N_ODD, D_MODEL), 0.02),
        'ffn_w_gate': nrm(ks[23], (DEPTH, D_MODEL, FFN_HIDDEN), D_MODEL ** -0.5),
        'ffn_w_up': nrm(ks[24], (DEPTH, D_MODEL, FFN_HIDDEN), D_MODEL ** -0.5),
        'ffn_w_down': nrm(ks[25], (DEPTH, FFN_HIDDEN, D_MODEL), FFN_HIDDEN ** -0.5),
    }


def reference(x, mix_pre_g, mix_post_g, ffn_pre_g, ffn_post_g, hgrn_lb_logits, even_w_in,
              hgrn_norm_g, ssd_conv_w, ssd_conv_b, ssd_dt_bias, ssd_a_log, ssd_d, ssd_norm_g,
              even_w_out, conf_w1, conf_b1, conf_dw_w, conf_dw_b, conf_ln_g, conf_ln_b,
              conf_w2, conf_b2, ffn_w_gate, ffn_w_up, ffn_w_down):
    lb_p = jax.nn.softmax(hgrn_lb_logits.astype(jnp.float32), axis=0)
    lower_bounds = jnp.cumsum(lb_p, axis=0) - lb_p[0]
    h = x
    for layer in range(DEPTH):
        i = layer // 2
        u = rms_norm(h, mix_pre_g[layer])
        if layer % 2 == 0:
            m = hgrn2_mamba2_mixer(u, even_w_in[i], lower_bounds[i], hgrn_norm_g[i],
                                   ssd_conv_w[i], ssd_conv_b[i], ssd_dt_bias[i], ssd_a_log[i],
                                   ssd_d[i], ssd_norm_g[i], even_w_out[i])
        else:
            m = conformer_conv_module(u, conf_w1[i], conf_b1[i], conf_dw_w[i], conf_dw_b[i],
                                      conf_ln_g[i], conf_ln_b[i], conf_w2[i], conf_b2[i])
        h = h + rms_norm(m, mix_post_g[layer])
        u = rms_norm(h, ffn_pre_g[layer])
        h = h + rms_norm(swiglu(u, ffn_w_gate[layer], ffn_w_up[layer], ffn_w_down[layer]),
                         ffn_post_g[layer])
    return h
```

```python
import functools

import jax
import jax.numpy as jnp
from jax import lax
from jax.experimental import pallas as pl
from jax.experimental.pallas import tpu as pltpu

F32 = jnp.float32
BF16 = jnp.bfloat16

RMS_EPS = 1e-6
LN_EPS = 1e-5
HGRN_F_MIN = 1e-6
HGRN_HEAD = 128
HGRN_CHUNK = 64
HGRN_SUB = 8
SSD_HEAD_DIM = 64
SSD_GROUPS = 4
SSD_STATE = 128
SSD_CONV = 4
SSD_CHUNK = 128
CONF_KERNEL = 31
CONF_HALO = 32
LANES = 128
VMEM_LIMIT = 56 * 1024 * 1024


def _cparams(semantics):
    return pltpu.CompilerParams(dimension_semantics=semantics,
                                vmem_limit_bytes=VMEM_LIMIT)


def _sigmoid(x):
    return 1.0 / (1.0 + jnp.exp(-x))


def _silu(x):
    return x * _sigmoid(x)


def _softplus(x):
    return jnp.maximum(x, 0.0) + jnp.log1p(jnp.exp(-jnp.abs(x)))


def _rms_rows(x, g, eps=RMS_EPS):
    ms = jnp.mean(x * x, axis=-1, keepdims=True)
    return x * lax.rsqrt(ms + eps) * g


def _dot(a, b):
    return jnp.dot(a, b, preferred_element_type=F32)


def _dot_nt(a, b):
    return lax.dot_general(a, b, (((1,), (1,)), ((), ())), preferred_element_type=F32)


def _dot_tn(a, b):
    return lax.dot_general(a, b, (((0,), (0,)), ((), ())), preferred_element_type=F32)


def _split3(x):
    hi = x.astype(BF16)
    r1 = x - hi.astype(F32)
    mid = r1.astype(BF16)
    lo = (r1 - mid.astype(F32)).astype(BF16)
    return hi, mid, lo


def _dot_exact_rhs(sel, x):
    hi, mid, lo = _split3(x)
    return _dot(sel, hi) + _dot(sel, mid) + _dot(sel, lo)


def _dot_exact_lhs(x, sel):
    hi, mid, lo = _split3(x)
    return _dot(hi, sel) + _dot(mid, sel) + _dot(lo, sel)


def _lower_tri(n, dtype):
    r = lax.broadcasted_iota(jnp.int32, (n, n), 0)
    c = lax.broadcasted_iota(jnp.int32, (n, n), 1)
    return (r >= c).astype(dtype)


def _prenorm_kernel(x_ref, g_ref, o_ref):
    o_ref[...] = _rms_rows(x_ref[...], g_ref[...]).astype(o_ref.dtype)


def prenorm(x, g, tm=512):
    m, d = x.shape
    tm = min(tm, m)
    return pl.pallas_call(
        _prenorm_kernel,
        grid=(m // tm,),
        in_specs=[pl.BlockSpec((tm, d), lambda i: (i, 0)),
                  pl.BlockSpec((1, d), lambda i: (0, 0))],
        out_specs=pl.BlockSpec((tm, d), lambda i: (i, 0)),
        out_shape=jax.ShapeDtypeStruct((m, d), BF16),
        compiler_params=_cparams(("arbitrary",)),
        name="prenorm",
    )(x, g.reshape(1, d))


def _wide_kernel(*refs, n_w, n_b, epilogue):
    u_ref = refs[0]
    w_refs = refs[1:1 + n_w]
    b_refs = refs[1 + n_w:1 + n_w + n_b]
    o_ref = refs[1 + n_w + n_b]
    s_refs = refs[2 + n_w + n_b:]

    @pl.when(pl.program_id(1) == 0)
    def _():
        for w, s in zip(w_refs, s_refs):
            s[...] = w[...].astype(BF16)

    u = u_ref[...]
    ys = [_dot(u, s[...]) for s in s_refs]
    o_ref[...] = epilogue(*ys, *[b[...] for b in b_refs]).astype(o_ref.dtype)


def wide_proj(u, ws, w_col_blocks, bs, b_col_blocks, n_blocks, epilogue, out_dtype,
              name, tn=512, tm=1024):
    m, k = u.shape
    tm = min(tm, m)
    in_specs = [pl.BlockSpec((tm, k), lambda n, i: (i, 0))]
    for off in w_col_blocks:
        in_specs.append(pl.BlockSpec((k, tn), lambda n, i, off=off: (0, n + off)))
    for off in b_col_blocks:
        in_specs.append(pl.BlockSpec((1, tn), lambda n, i, off=off: (0, n + off)))
    return pl.pallas_call(
        functools.partial(_wide_kernel, n_w=len(ws), n_b=len(bs), epilogue=epilogue),
        grid=(n_blocks, m // tm),
        in_specs=in_specs,
        out_specs=pl.BlockSpec((tm, tn), lambda n, i: (i, n)),
        out_shape=jax.ShapeDtypeStruct((m, n_blocks * tn), out_dtype),
        scratch_shapes=[pltpu.VMEM((k, tn), BF16) for _ in ws],
        compiler_params=_cparams(("arbitrary", "arbitrary")),
        name=name,
    )(u, *ws, *bs)


def _identity_epilogue(y):
    return y


def _swiglu_epilogue(g, up):
    return _silu(g) * up


def _glu_epilogue(a, g, ba, bg):
    return (a + ba) * _sigmoid(g + bg)


def _small_proj_kernel(u_ref, w_ref, o_ref):
    o_ref[...] = _dot(u_ref[...], w_ref[...])


def small_proj(u, w_bf16, tm=1024):
    m, k = u.shape
    n = w_bf16.shape[1]
    tm = min(tm, m)
    return pl.pallas_call(
        _small_proj_kernel,
        grid=(m // tm,),
        in_specs=[pl.BlockSpec((tm, k), lambda i: (i, 0)),
                  pl.BlockSpec((k, n), lambda i: (0, 0))],
        out_specs=pl.BlockSpec((tm, n), lambda i: (i, 0)),
        out_shape=jax.ShapeDtypeStruct((m, n), F32),
        compiler_params=_cparams(("arbitrary",)),
        name="dt_proj",
    )(u, w_bf16)


def _residual_epilogue(m, h, gpost, gnext, hout_ref, unext_ref):
    hn = h + _rms_rows(m, gpost)
    hout_ref[...] = hn
    if unext_ref is not None:
        unext_ref[...] = _rms_rows(hn, gnext).astype(unext_ref.dtype)


def _out_proj_kernel(a_ref, w_ref, h_ref, gpost_ref, gnext_ref, *rest, n_k, with_next):
    if with_next:
        hout_ref, unext_ref, acc_ref = rest
    else:
        (hout_ref, acc_ref), unext_ref = rest, None
    k = pl.program_id(1)

    @pl.when(k == 0)
    def _():
        acc_ref[...] = jnp.zeros_like(acc_ref)

    acc_ref[...] += _dot(a_ref[...], w_ref[...])

    @pl.when(k == n_k - 1)
    def _():
        _residual_epilogue(acc_ref[...], h_ref[...], gpost_ref[...], gnext_ref[...],
                           hout_ref, unext_ref)


def out_proj(a, w_bf16, h, gpost, gnext, with_next, name, tm=512, tk=512):
    m, kk = a.shape
    d = w_bf16.shape[1]
    tm = min(tm, m)
    tk = min(tk, kk)
    n_k = kk // tk
    row = pl.BlockSpec((tm, d), lambda i, k: (i, 0))
    vec = pl.BlockSpec((1, d), lambda i, k: (0, 0))
    out_shape = [jax.ShapeDtypeStruct((m, d), F32)]
    out_specs = [row]
    if with_next:
        out_shape.append(jax.ShapeDtypeStruct((m, d), BF16))
        out_specs.append(row)
    res = pl.pallas_call(
        functools.partial(_out_proj_kernel, n_k=n_k, with_next=with_next),
        grid=(m // tm, n_k),
        in_specs=[pl.BlockSpec((tm, tk), lambda i, k: (i, k)),
                  pl.BlockSpec((tk, d), lambda i, k: (k, 0)),
                  row, vec, vec],
        out_specs=out_specs,
        out_shape=out_shape,
        scratch_shapes=[pltpu.VMEM((tm, d), F32)],
        compiler_params=_cparams(("arbitrary", "arbitrary")),
        name=name,
    )(a, w_bf16, h, gpost.reshape(1, d), gnext.reshape(1, d))
    return (res[0], res[1]) if with_next else (res[0], None)


def _hgrn_kernel(q_ref, f_ref, v_ref, gate_ref, lb_ref, an_ref, o_ref,
                 st_ref, kp_ref, bp_ref, *, n_chunks):
    C, SUB, HD = HGRN_CHUNK, HGRN_SUB, HGRN_HEAD

    @pl.when(pl.program_id(2) == 0)
    def _():
        st_ref[...] = jnp.zeros_like(st_ref)

    kp_ref[0:SUB, :] = jnp.zeros((SUB, HD), F32)
    bp_ref[0:SUB, :] = jnp.zeros((SUB, HD), F32)

    lb = lb_ref[...]
    an = an_ref[...]
    tri = _lower_tri(C, BF16)
    ones = jnp.ones((HD, HD), BF16)
    rows = lax.broadcasted_iota(jnp.int32, (C, 1), 0)
    row_in_sub = rows % SUB
    rr = lax.broadcasted_iota(jnp.int32, (C, C), 0)
    cc = lax.broadcasted_iota(jnp.int32, (C, C), 1)

    def chunk(c, carry):
        r0 = pl.multiple_of(c * C, C)
        sl = pl.ds(r0, C)
        q_raw = q_ref[sl, :]
        sig = _sigmoid(f_ref[sl, :])
        v = v_ref[sl, :]
        f = lb + (1.0 - lb) * sig
        k = (1.0 - lb) * (1.0 - sig)
        log_f = jnp.log(jnp.maximum(f, HGRN_F_MIN))
        qs = _silu(q_raw)
        b = _dot_exact_rhs(tri, log_f)
        b_last = b[C - 1:C, :]
        v16 = v.astype(BF16)

        st = st_ref[...]
        o = _dot_nt((qs * jnp.exp(b)).astype(BF16), st.astype(BF16))
        k_end = k * jnp.exp(b_last - b)
        st_ref[...] = st * jnp.exp(b_last) + _dot_tn(v16, k_end.astype(BF16))

        blocks = [jnp.zeros((SUB, C), F32)]
        for i in range(1, C // SUB):
            i0 = i * SUB
            ref_b = b[i0 - 1:i0, :]
            qt = qs[i0:i0 + SUB, :] * jnp.exp(b[i0:i0 + SUB, :] - ref_b)
            kt = jnp.where(rows < i0, k * jnp.exp(jnp.minimum(ref_b - b, 0.0)), 0.0)
            blocks.append(_dot_nt(qt.astype(BF16), kt.astype(BF16)))
        scores = jnp.concatenate(blocks, axis=0)

        kp_ref[SUB:SUB + C, :] = k
        bp_ref[SUB:SUB + C, :] = b
        for d in range(SUB):
            ks = kp_ref[pl.ds(SUB - d, C), :]
            bs = bp_ref[pl.ds(SUB - d, C), :]
            valid = row_in_sub >= d
            e = jnp.where(valid, qs * ks * jnp.exp(jnp.where(valid, b - bs, 0.0)), 0.0)
            rsum = _dot(e.astype(BF16), ones)
            scores = scores + jnp.where(cc == rr - d, rsum[:, 0:C], 0.0)

        o = o + _dot(scores.astype(BF16), v16)
        y = _rms_rows(o, an) * _silu(gate_ref[sl, :])
        o_ref[sl, :] = y.astype(o_ref.dtype)
        return carry

    lax.fori_loop(0, n_chunks, chunk, 0)


def hgrn_mix(proj, lb, a_norm, batch, seq, d_model, rows_per_step=512):
    n_heads = d_model // HGRN_HEAD
    cs = min(rows_per_step, seq)
    steps = seq // cs
    m = batch * seq

    def col(seg):
        return pl.BlockSpec((cs, HGRN_HEAD),
                            lambda b, h, s, seg=seg: (b * steps + s, seg * n_heads + h))

    vec = pl.BlockSpec((1, HGRN_HEAD), lambda b, h, s: (0, h))
    return pl.pallas_call(
        functools.partial(_hgrn_kernel, n_chunks=cs // HGRN_CHUNK),
        grid=(batch, n_heads, steps),
        in_specs=[col(0), col(1), col(2), col(3), vec, vec],
        out_specs=pl.BlockSpec((cs, HGRN_HEAD), lambda b, h, s: (b * steps + s, h)),
        out_shape=jax.ShapeDtypeStruct((m, d_model), BF16),
        scratch_shapes=[pltpu.VMEM((HGRN_HEAD, HGRN_HEAD), F32),
                        pltpu.VMEM((HGRN_SUB + HGRN_CHUNK, HGRN_HEAD), F32),
                        pltpu.VMEM((HGRN_SUB + HGRN_CHUNK, HGRN_HEAD), F32)],
        compiler_params=_cparams(("arbitrary", "arbitrary", "arbitrary")),
        name="hgrn2",
    )(proj, proj, proj, proj, lb.reshape(1, d_model), a_norm.reshape(1, d_model))


def _causal_conv(raw_ref, carry_ref, ext_ref, w_ref, b_ref, n_rows, n_taps, halo):
    ext_ref[0:halo, :] = carry_ref[...]
    ext_ref[halo:halo + n_rows, :] = raw_ref[...]
    carry_ref[...] = raw_ref[n_rows - halo:n_rows, :]
    acc = b_ref[...] + w_ref[0:1, :] * ext_ref[pl.ds(halo - n_taps + 1, n_rows), :]
    for k in range(1, n_taps):
        acc = acc + w_ref[k:k + 1, :] * ext_ref[pl.ds(halo - n_taps + 1 + k, n_rows), :]
    return acc


def _ssd_kernel(x_ref, bm_ref, cm_ref, z_ref, dt_ref,
                cwx_ref, cwb_ref, cwc_ref, cbx_ref, cbb_ref, cbc_ref,
                dtb_ref, alog_ref, dskip_ref, norm_ref, o_ref,
                st_ref, carx_ref, carb_ref, carc_ref, extx_ref, extb_ref, extc_ref,
                *, heads_per_group):
    L, P, R = SSD_CHUNK, SSD_HEAD_DIM, heads_per_group
    gw = R * P
    g = pl.program_id(1)

    @pl.when(pl.program_id(2) == 0)
    def _():
        st_ref[...] = jnp.zeros_like(st_ref)
        carx_ref[...] = jnp.zeros_like(carx_ref)
        carb_ref[...] = jnp.zeros_like(carb_ref)
        carc_ref[...] = jnp.zeros_like(carc_ref)

    halo = carx_ref.shape[0]
    xc = _silu(_causal_conv(x_ref, carx_ref, extx_ref, cwx_ref, cbx_ref, L, SSD_CONV, halo))
    bc = _silu(_causal_conv(bm_ref, carb_ref, extb_ref, cwb_ref, cbb_ref, L, SSD_CONV, halo))
    cc = _silu(_causal_conv(cm_ref, carc_ref, extc_ref, cwc_ref, cbc_ref, L, SSD_CONV, halo))
    bc16 = bc.astype(BF16)
    cc16 = cc.astype(BF16)

    dt = _softplus(dt_ref[...] + dtb_ref[...])
    dta = dt * (-jnp.exp(alog_ref[...]))
    cs = _dot_exact_rhs(_lower_tri(L, BF16), dta)

    def expand(width):
        r = lax.broadcasted_iota(jnp.int32, (LANES, R * width), 0)
        c = lax.broadcasted_iota(jnp.int32, (LANES, R * width), 1)
        return (r == g * R + c // width).astype(BF16)

    sel_p = expand(P)
    dt_x = _dot_exact_lhs(dt, sel_p)
    cs_x = _dot_exact_lhs(cs, sel_p)
    cs_w = _dot_exact_lhs(cs, expand(L))

    xdt = xc * dt_x
    cs_last = cs_x[L - 1:L, :]
    st = st_ref[...]
    y = _dot(cc16, st.astype(BF16)) * jnp.exp(cs_x)
    st_ref[...] = st * jnp.exp(cs_last) + _dot_tn(
        bc16, (xdt * jnp.exp(cs_last - cs_x)).astype(BF16))

    cb = _dot_nt(cc16, bc16)
    rr = lax.broadcasted_iota(jnp.int32, (L, L), 0)
    cl = lax.broadcasted_iota(jnp.int32, (L, L), 1)
    causal = rr >= cl
    xdt16 = xdt.astype(BF16)
    parts = []
    for r in range(R):
        col = cs_w[:, r * L:(r + 1) * L]
        seg = col - col.T
        decay = jnp.where(causal, jnp.exp(jnp.where(causal, seg, 0.0)), 0.0)
        parts.append(_dot((cb * decay).astype(BF16), xdt16[:, r * P:(r + 1) * P]))
    y = y + jnp.concatenate(parts, axis=1)

    y = y + dskip_ref[...] * xc
    yz = y * _silu(z_ref[...])
    o_ref[...] = _rms_rows(yz, norm_ref[...]).astype(o_ref.dtype)


def ssd_mix(proj, dt_raw, conv_w, conv_b, dt_bias, a_log, d_skip, b_norm,
            batch, seq, d_model):
    L, G, N = SSD_CHUNK, SSD_GROUPS, SSD_STATE
    n_heads = d_model // SSD_HEAD_DIM
    R = n_heads // G
    gw = d_model // G
    steps = seq // L
    m = batch * seq
    halo = 8
    bn0 = 6 * d_model // N
    cwb0 = d_model // N

    def pad_heads(v):
        return jnp.pad(v.astype(F32), (0, LANES - n_heads)).reshape(1, LANES)

    def rep_heads(v):
        return jnp.repeat(v.astype(F32), SSD_HEAD_DIM).reshape(1, d_model)

    row_g = lambda off: pl.BlockSpec((L, gw), lambda b, g, s, off=off: (b * steps + s, off + g))
    row_n = lambda off: pl.BlockSpec((L, N), lambda b, g, s, off=off: (b * steps + s, off + g))
    par_g = lambda rows: pl.BlockSpec((rows, gw), lambda b, g, s: (0, g))
    par_n = lambda rows, off: pl.BlockSpec((rows, N), lambda b, g, s, off=off: (0, off + g))
    lane_vec = pl.BlockSpec((1, LANES), lambda b, g, s: (0, 0))
    cb2 = conv_b.reshape(1, -1)

    return pl.pallas_call(
        functools.partial(_ssd_kernel, heads_per_group=R),
        grid=(batch, G, steps),
        in_specs=[row_g(5 * G), row_n(bn0), row_n(bn0 + G), row_g(4 * G),
                  pl.BlockSpec((L, LANES), lambda b, g, s: (b * steps + s, 0)),
                  par_g(SSD_CONV), par_n(SSD_CONV, cwb0), par_n(SSD_CONV, cwb0 + G),
                  par_g(1), par_n(1, cwb0), par_n(1, cwb0 + G),
                  lane_vec, lane_vec, par_g(1), par_g(1)],
        out_specs=pl.BlockSpec((L, gw), lambda b, g, s: (b * steps + s, g)),
        out_shape=jax.ShapeDtypeStruct((m, d_model), BF16),
        scratch_shapes=[pltpu.VMEM((N, gw), F32),
                        pltpu.VMEM((halo, gw), F32), pltpu.VMEM((halo, N), F32),
                        pltpu.VMEM((halo, N), F32),
                        pltpu.VMEM((halo + L, gw), F32), pltpu.VMEM((halo + L, N), F32),
                        pltpu.VMEM((halo + L, N), F32)],
        compiler_params=_cparams(("arbitrary", "arbitrary", "arbitrary")),
        name="ssd",
    )(proj, proj, proj, proj, dt_raw,
      conv_w, conv_w, conv_w, cb2, cb2, cb2,
      pad_heads(dt_bias), pad_heads(a_log), rep_heads(d_skip), b_norm.reshape(1, d_model))


def _conf_tail_kernel(c_ref, dww_ref, dwb_ref, lng_ref, lnb_ref, w2_ref, b2_ref,
                      h_ref, gpost_ref, gnext_ref, hout_ref, unext_ref,
                      carry_ref, ext_ref, conv_ref, *, row_block):
    tm, d = c_ref.shape
    halo = CONF_HALO
    first = halo - (CONF_KERNEL - 1)

    @pl.when(pl.program_id(1) == 0)
    def _():
        carry_ref[...] = jnp.zeros_like(carry_ref)

    ext_ref[0:halo, :] = carry_ref[...]
    ext_ref[halo:halo + tm, :] = c_ref[...]
    carry_ref[...] = c_ref[tm - halo:tm, :]

    def col_tile(j, carry):
        cj = pl.ds(pl.multiple_of(j * LANES, LANES), LANES)
        for rb in range(tm // row_block):
            base = rb * row_block
            acc = dwb_ref[:, cj] + dww_ref[0:1, cj] * ext_ref[pl.ds(base + first, row_block), cj]
            for k in range(1, CONF_KERNEL):
                acc = acc + dww_ref[k:k + 1, cj] * ext_ref[pl.ds(base + first + k, row_block), cj]
            conv_ref[base:base + row_block, cj] = acc
        return carry

    lax.fori_loop(0, d // LANES, col_tile, 0)

    x = conv_ref[...]
    mu = jnp.mean(x, axis=-1, keepdims=True)
    xc = x - mu
    var = jnp.mean(xc * xc, axis=-1, keepdims=True)
    y = _silu(xc * lax.rsqrt(var + LN_EPS) * lng_ref[...] + lnb_ref[...])
    m = _dot(y.astype(BF16), w2_ref[...]) + b2_ref[...]
    _residual_epilogue(m, h_ref[...], gpost_ref[...], gnext_ref[...], hout_ref, unext_ref)


def conf_tail(c, dw_w, dw_b, ln_g, ln_b, w2_bf16, b2, h, gpost, gnext, batch, seq,
              tm=256, row_block=64):
    m, d = c.shape
    tm = min(tm, seq)
    row_block = min(row_block, tm)
    steps = seq // tm
    row = pl.BlockSpec((tm, d), lambda b, s: (b * steps + s, 0))
    vec = pl.BlockSpec((1, d), lambda b, s: (0, 0))
    dww = jnp.pad(dw_w, ((0, CONF_HALO - CONF_KERNEL), (0, 0)))
    return pl.pallas_call(
        functools.partial(_conf_tail_kernel, row_block=row_block),
        grid=(batch, steps),
        in_specs=[row, pl.BlockSpec((CONF_HALO, d), lambda b, s: (0, 0)), vec, vec, vec,
                  pl.BlockSpec((d, d), lambda b, s: (0, 0)), vec, row, vec, vec],
        out_specs=[row, row],
        out_shape=[jax.ShapeDtypeStruct((m, d), F32), jax.ShapeDtypeStruct((m, d), BF16)],
        scratch_shapes=[pltpu.VMEM((CONF_HALO, d), F32),
                        pltpu.VMEM((CONF_HALO + tm, d), F32),
                        pltpu.VMEM((tm, d), F32)],
        compiler_params=_cparams(("arbitrary", "arbitrary")),
        name="conf_tail",
    )(c, dww, dw_b.reshape(1, d), ln_g.reshape(1, d), ln_b.reshape(1, d), w2_bf16,
      b2.reshape(1, d), h, gpost.reshape(1, d), gnext.reshape(1, d))


def kernel(x, mix_pre_g, mix_post_g, ffn_pre_g, ffn_post_g, hgrn_lb_logits, even_w_in,
           hgrn_norm_g, ssd_conv_w, ssd_conv_b, ssd_dt_bias, ssd_a_log, ssd_d, ssd_norm_g,
           even_w_out, conf_w1, conf_b1, conf_dw_w, conf_dw_b, conf_ln_g, conf_ln_b,
           conf_w2, conf_b2, ffn_w_gate, ffn_w_up, ffn_w_down):
    batch, seq, d = x.shape
    depth = mix_pre_g.shape[0]
    hidden = ffn_w_gate.shape[2]
    m = batch * seq
    tn = 512
    main_cols = 6 * d + 2 * SSD_GROUPS * SSD_STATE
    n_ssd_heads = d // SSD_HEAD_DIM

    lb_p = jax.nn.softmax(hgrn_lb_logits.astype(F32), axis=0)
    lower_bounds = jnp.cumsum(lb_p, axis=0) - lb_p[0]

    h = x.reshape(m, d)
    u = prenorm(h, mix_pre_g[0])
    for layer in range(depth):
        i = layer // 2
        if layer % 2 == 0:
            w_in = even_w_in[i]
            proj = wide_proj(u, [w_in], [0], [], [], main_cols // tn, _identity_epilogue,
                             F32, "even_in_proj", tn=tn)
            w_dt = jnp.pad(w_in[:, main_cols:], ((0, 0), (0, LANES - n_ssd_heads))).astype(BF16)
            dt_raw = small_proj(u, w_dt)
            o_a = hgrn_mix(proj, lower_bounds[i], hgrn_norm_g[i], batch, seq, d)
            o_b = ssd_mix(proj, dt_raw, ssd_conv_w[i], ssd_conv_b[i], ssd_dt_bias[i],
                          ssd_a_log[i], ssd_d[i], ssd_norm_g[i], batch, seq, d)
            mixed = jnp.concatenate([o_a, o_b], axis=-1)
            h, u = out_proj(mixed, even_w_out[i].astype(BF16), h, mix_post_g[layer],
                            ffn_pre_g[layer], True, "even_out_proj")
        else:
            c = wide_proj(u, [conf_w1[i], conf_w1[i]], [0, d // tn],
                          [conf_b1[i].reshape(1, -1)] * 2, [0, d // tn], d // tn,
                          _glu_epilogue, F32, "conf_glu", tn=tn)
            h, u = conf_tail(c, conf_dw_w[i], conf_dw_b[i], conf_ln_g[i], conf_ln_b[i],
                             conf_w2[i].astype(BF16), conf_b2[i], h, mix_post_g[layer],
                             ffn_pre_g[layer], batch, seq)
        act = wide_proj(u, [ffn_w_gate[layer], ffn_w_up[layer]], [0, 0], [], [],
                        hidden // tn, _swiglu_epilogue, BF16, "ffn_in", tn=tn)
        last = layer == depth - 1
        gnext = mix_pre_g[layer] if last else mix_pre_g[layer + 1]
        h, u = out_proj(act, ffn_w_down[layer].astype(BF16), h, ffn_post_g[layer],
                        gnext, not last, "ffn_out")
    return h.reshape(batch, seq, d)
```

```python
import functools

import jax
import jax.numpy as jnp
from jax import lax
from jax.experimental import pallas as pl
from jax.experimental.pallas import tpu as pltpu

F32 = jnp.float32
BF16 = jnp.bfloat16

RMS_EPS = 1e-6
LN_EPS = 1e-5
HGRN_F_MIN = 1e-6
HGRN_HEAD = 128
HGRN_CHUNK = 64
HGRN_SUB = 8
HGRN_HEADS_PER_PART = 4
HGRN_PARTS_PER_STEP = 2
SSD_HEAD_DIM = 64
SSD_GROUPS = 4
SSD_STATE = 128
SSD_CONV = 4
SSD_CHUNK = 128
CONF_KERNEL = 31
CONF_HALO = 32
LANES = 128
VMEM_LIMIT = 56 * 1024 * 1024


def _cparams(semantics):
    return pltpu.CompilerParams(dimension_semantics=semantics,
                                vmem_limit_bytes=VMEM_LIMIT)


def _sigmoid(x):
    return 0.5 * jnp.tanh(0.5 * x) + 0.5


def _silu(x):
    return x * _sigmoid(x)


def _softplus(x):
    return jnp.maximum(x, 0.0) + jnp.log1p(jnp.exp(-jnp.abs(x)))


def _rms_rows(x, g, eps=RMS_EPS):
    ms = jnp.mean(x * x, axis=-1, keepdims=True)
    return x * lax.rsqrt(ms + eps) * g


def _dot(a, b):
    return jnp.dot(a, b, preferred_element_type=F32)


def _dot_nt(a, b):
    return lax.dot_general(a, b, (((1,), (1,)), ((), ())), preferred_element_type=F32)


def _dot_tn(a, b):
    return lax.dot_general(a, b, (((0,), (0,)), ((), ())), preferred_element_type=F32)


def _split3(x):
    hi = x.astype(BF16)
    r1 = x - hi.astype(F32)
    mid = r1.astype(BF16)
    lo = (r1 - mid.astype(F32)).astype(BF16)
    return hi, mid, lo


def _dot_exact_rhs(sel, x):
    hi, mid, lo = _split3(x)
    return _dot(sel, hi) + _dot(sel, mid) + _dot(sel, lo)


def _dot_exact_lhs(x, sel):
    hi, mid, lo = _split3(x)
    return _dot(hi, sel) + _dot(mid, sel) + _dot(lo, sel)


def _lower_tri(n, dtype):
    r = lax.broadcasted_iota(jnp.int32, (n, n), 0)
    c = lax.broadcasted_iota(jnp.int32, (n, n), 1)
    return (r >= c).astype(dtype)


def _prenorm_kernel(x_ref, g_ref, o_ref):
    o_ref[...] = _rms_rows(x_ref[...], g_ref[...]).astype(o_ref.dtype)


def prenorm(x, g, tm=512):
    m, d = x.shape
    tm = min(tm, m)
    return pl.pallas_call(
        _prenorm_kernel,
        grid=(m // tm,),
        in_specs=[pl.BlockSpec((tm, d), lambda i: (i, 0)),
                  pl.BlockSpec((1, d), lambda i: (0, 0))],
        out_specs=pl.BlockSpec((tm, d), lambda i: (i, 0)),
        out_shape=jax.ShapeDtypeStruct((m, d), BF16),
        compiler_params=_cparams(("arbitrary",)),
        name="prenorm",
    )(x, g.reshape(1, d))


def _wide_kernel(*refs, n_w, n_b, epilogue):
    u_ref = refs[0]
    w_refs = refs[1:1 + n_w]
    b_refs = refs[1 + n_w:1 + n_w + n_b]
    o_ref = refs[1 + n_w + n_b]
    s_refs = refs[2 + n_w + n_b:]

    @pl.when(pl.program_id(1) == 0)
    def _():
        for w, s in zip(w_refs, s_refs):
            s[...] = w[...].astype(BF16)

    u = u_ref[...]
    ys = [_dot(u, s[...]) for s in s_refs]
    o_ref[...] = epilogue(*ys, *[b[...] for b in b_refs]).astype(o_ref.dtype)


def wide_proj(u, ws, layer, w_col_maps, bs, b_col_maps, n_blocks, epilogue, out_dtype,
              name, tn=512, tm=1024):
    m, k = u.shape
    tm = min(tm, m)
    in_specs = [pl.BlockSpec((tm, k), lambda n, i: (i, 0))]
    for cmap in w_col_maps:
        in_specs.append(pl.BlockSpec((None, k, tn), lambda n, i, cmap=cmap: (layer, 0, cmap(n))))
    for cmap in b_col_maps:
        in_specs.append(pl.BlockSpec((None, 1, tn), lambda n, i, cmap=cmap: (layer, 0, cmap(n))))
    return pl.pallas_call(
        functools.partial(_wide_kernel, n_w=len(ws), n_b=len(bs), epilogue=epilogue),
        grid=(n_blocks, m // tm),
        in_specs=in_specs,
        out_specs=pl.BlockSpec((tm, tn), lambda n, i: (i, n)),
        out_shape=jax.ShapeDtypeStruct((m, n_blocks * tn), out_dtype),
        scratch_shapes=[pltpu.VMEM((k, tn), BF16) for _ in ws],
        compiler_params=_cparams(("arbitrary", "arbitrary")),
        name=name,
    )(u, *ws, *bs)


def _identity_epilogue(y):
    return y


def _swiglu_epilogue(g, up):
    return _silu(g) * up


def _glu_epilogue(a, g, ba, bg):
    return (a + ba) * _sigmoid(g + bg)


def _small_proj_kernel(u_ref, w_ref, o_ref):
    o_ref[...] = _dot(u_ref[...], w_ref[...])


def small_proj(u, w_bf16, tm=1024):
    m, k = u.shape
    n = w_bf16.shape[1]
    tm = min(tm, m)
    return pl.pallas_call(
        _small_proj_kernel,
        grid=(m // tm,),
        in_specs=[pl.BlockSpec((tm, k), lambda i: (i, 0)),
                  pl.BlockSpec((k, n), lambda i: (0, 0))],
        out_specs=pl.BlockSpec((tm, n), lambda i: (i, 0)),
        out_shape=jax.ShapeDtypeStruct((m, n), F32),
        compiler_params=_cparams(("arbitrary",)),
        name="dt_proj",
    )(u, w_bf16)


def _residual_epilogue(m, h, gpost, gnext, hout_ref, unext_ref):
    hn = h + _rms_rows(m, gpost)
    hout_ref[...] = hn
    if unext_ref is not None:
        unext_ref[...] = _rms_rows(hn, gnext).astype(unext_ref.dtype)


def _out_proj_kernel(*refs, n_a, with_next):
    a_refs = refs[:n_a]
    w_ref, h_ref, gpost_ref, gnext_ref, hout_ref = refs[n_a:n_a + 5]
    unext_ref = refs[n_a + 5] if with_next else None
    off = 0
    m = None
    for a_ref in a_refs:
        ka = a_ref.shape[1]
        part = _dot(a_ref[...], w_ref[off:off + ka, :])
        m = part if m is None else m + part
        off += ka
    _residual_epilogue(m, h_ref[...], gpost_ref[...], gnext_ref[...], hout_ref, unext_ref)


def out_proj(acts, w_stack_bf16, layer, h, gpost, gnext, with_next, name, tm=256):
    m = h.shape[0]
    _, kk, d = w_stack_bf16.shape
    tm = min(tm, m)
    row = pl.BlockSpec((tm, d), lambda i: (i, 0))
    vec = pl.BlockSpec((1, d), lambda i: (0, 0))
    out_shape = [jax.ShapeDtypeStruct((m, d), F32)]
    out_specs = [row]
    if with_next:
        out_shape.append(jax.ShapeDtypeStruct((m, d), BF16))
        out_specs.append(row)
    res = pl.pallas_call(
        functools.partial(_out_proj_kernel, n_a=len(acts), with_next=with_next),
        grid=(m // tm,),
        in_specs=[pl.BlockSpec((tm, a.shape[1]), lambda i: (i, 0)) for a in acts] + [
            pl.BlockSpec((None, kk, d), lambda i: (layer, 0, 0),
                         pipeline_mode=pl.Buffered(1)),
            row, vec, vec],
        out_specs=out_specs,
        out_shape=out_shape,
        compiler_params=_cparams(("arbitrary",)),
        name=name,
    )(*acts, w_stack_bf16, h, gpost.reshape(1, d), gnext.reshape(1, d))
    return (res[0], res[1]) if with_next else (res[0], None)


def _hgrn_kernel(q_ref, f_ref, v_ref, gate_ref, lb_ref, an_ref, o_ref, st_ref,
                 *, n_chunks, n_heads, part_heads):
    C, SUB, HD = HGRN_CHUNK, HGRN_SUB, HGRN_HEAD
    n_sub = C // SUB
    mid = SUB // 2 - 1
    width = part_heads * HD
    head_lanes = [slice(h * HD, (h + 1) * HD) for h in range(part_heads)]
    parts = [slice(p * width, (p + 1) * width) for p in range(n_heads // part_heads)]

    @pl.when(pl.program_id(2) == 0)
    def _():
        st_ref[...] = jnp.zeros_like(st_ref)

    tri = _lower_tri(C, BF16)
    causal = (lax.broadcasted_iota(jnp.int32, (C, C), 0)
              >= lax.broadcasted_iota(jnp.int32, (C, C), 1))
    def gates(sl, part):
        lb = lb_ref[:, part]
        sig = _sigmoid(f_ref[sl, part])
        f = lb + (1.0 - lb) * sig
        k = (1.0 - lb) * (1.0 - sig)
        log_f = jnp.log(jnp.maximum(f, HGRN_F_MIN))
        qs = _silu(q_ref[sl, part].astype(F32))
        b = _dot_exact_rhs(tri, log_f)
        return qs, k, b, v_ref[sl, part]

    def scores_and_state(vals, h0):
        qs, k, b, v16 = vals
        b_last = b[C - 1:C, :]

        q_in16 = (qs * jnp.exp(b)).astype(BF16)
        k_end16 = (k * jnp.exp(b_last - b)).astype(BF16)
        dec_last = jnp.exp(b_last)
        sts = [st_ref[h0 + h] for h in range(part_heads)]
        o_heads = [_dot_nt(q_in16[:, s], st.astype(BF16)) for s, st in zip(head_lanes, sts)]
        for h, (s, st) in enumerate(zip(head_lanes, sts)):
            st_ref[h0 + h] = st * dec_last[:, s] + _dot_tn(v16[:, s], k_end16[:, s])

        blocks = [[] for _ in range(part_heads)]
        keys = []
        prev_ref = None
        for i in range(n_sub):
            rows = slice(i * SUB, (i + 1) * SUB)
            ref_b = b[i * SUB + mid:i * SUB + mid + 1, :]
            if keys:
                step = jnp.exp(ref_b - prev_ref)
                keys = [kj * step for kj in keys]
            keys.append(k[rows, :] * jnp.exp(ref_b - b[rows, :]))
            prev_ref = ref_b
            qt16 = (qs[rows, :] * jnp.exp(b[rows, :] - ref_b)).astype(BF16)
            pad = [jnp.zeros(((n_sub - 1 - i) * SUB, width), F32)] if i < n_sub - 1 else []
            kt16 = jnp.concatenate(keys + pad, axis=0).astype(BF16)
            for h, s in enumerate(head_lanes):
                blocks[h].append(_dot_nt(qt16[:, s], kt16[:, s]))
        return o_heads, blocks

    def outputs(sl, part, vals, o_heads, blocks):
        v16 = vals[3]
        for h, s in enumerate(head_lanes):
            scores = jnp.where(causal, jnp.concatenate(blocks[h], axis=0), 0.0)
            o_heads[h] = o_heads[h] + _dot(scores.astype(BF16), v16[:, s])
        y = jnp.concatenate(
            [o * lax.rsqrt(jnp.mean(o * o, axis=-1, keepdims=True) + RMS_EPS) for o in o_heads],
            axis=1)
        y = y * an_ref[:, part] * _silu(gate_ref[sl, part].astype(F32))
        o_ref[sl, part] = y.astype(o_ref.dtype)

    def chunk(c, carry):
        sl = pl.ds(pl.multiple_of(c * C, C), C)
        vals = [gates(sl, part) for part in parts]
        mids = [scores_and_state(v, p * part_heads) for p, v in enumerate(vals)]
        for part, v, (o_heads, blocks) in zip(parts, vals, mids):
            outputs(sl, part, v, o_heads, blocks)
        return carry

    lax.fori_loop(0, n_chunks, chunk, 0)


def hgrn_mix(pb, f_pre, lb, a_norm, batch, seq, d_model, rows_per_step=512):
    n_heads = d_model // HGRN_HEAD
    part_heads = min(HGRN_HEADS_PER_PART, n_heads)
    hb = min(HGRN_PARTS_PER_STEP * part_heads, n_heads)
    gw = hb * HGRN_HEAD
    n_groups = n_heads // hb
    cs = min(rows_per_step, seq)
    steps = seq // cs
    m = batch * seq

    def col(seg):
        return pl.BlockSpec((cs, gw), lambda b, h, s, seg=seg: (b * steps + s, seg * n_groups + h))

    vec = pl.BlockSpec((1, gw), lambda b, h, s: (0, h))
    return pl.pallas_call(
        functools.partial(_hgrn_kernel, n_chunks=cs // HGRN_CHUNK, n_heads=hb,
                          part_heads=part_heads),
        grid=(batch, n_groups, steps),
        in_specs=[col(0), col(0), col(1), col(2), vec, vec],
        out_specs=col(0),
        out_shape=jax.ShapeDtypeStruct((m, d_model), BF16),
        scratch_shapes=[pltpu.VMEM((hb, HGRN_HEAD, HGRN_HEAD), F32)],
        compiler_params=_cparams(("arbitrary", "arbitrary", "arbitrary")),
        name="hgrn2",
    )(pb, f_pre, pb, pb, lb.reshape(1, d_model), a_norm.reshape(1, d_model))


def _causal_conv(raw_ref, carry_ref, ext_ref, w_ref, b_ref, n_rows, n_taps, halo):
    raw = raw_ref[...].astype(F32)
    ext_ref[0:halo, :] = carry_ref[...]
    ext_ref[halo:halo + n_rows, :] = raw
    carry_ref[...] = raw[n_rows - halo:n_rows, :]
    acc = b_ref[...] + w_ref[0:1, :] * ext_ref[pl.ds(halo - n_taps + 1, n_rows), :]
    for k in range(1, n_taps):
        acc = acc + w_ref[k:k + 1, :] * ext_ref[pl.ds(halo - n_taps + 1 + k, n_rows), :]
    return acc


def _ssd_kernel(x_ref, bm_ref, cm_ref, z_ref, dt_ref,
                cwx_ref, cwb_ref, cwc_ref, cbx_ref, cbb_ref, cbc_ref,
                dtb_ref, alog_ref, dskip_ref, norm_ref, o_ref,
                st_ref, carx_ref, carb_ref, carc_ref, extx_ref, extb_ref, extc_ref,
                *, heads_per_group):
    L, P, R = SSD_CHUNK, SSD_HEAD_DIM, heads_per_group
    g = pl.program_id(1)

    @pl.when(pl.program_id(2) == 0)
    def _():
        st_ref[...] = jnp.zeros_like(st_ref)
        carx_ref[...] = jnp.zeros_like(carx_ref)
        carb_ref[...] = jnp.zeros_like(carb_ref)
        carc_ref[...] = jnp.zeros_like(carc_ref)

    halo = carx_ref.shape[0]
    xc = _silu(_causal_conv(x_ref, carx_ref, extx_ref, cwx_ref, cbx_ref, L, SSD_CONV, halo))
    bc = _silu(_causal_conv(bm_ref, carb_ref, extb_ref, cwb_ref, cbb_ref, L, SSD_CONV, halo))
    cc = _silu(_causal_conv(cm_ref, carc_ref, extc_ref, cwc_ref, cbc_ref, L, SSD_CONV, halo))
    bc16 = bc.astype(BF16)
    cc16 = cc.astype(BF16)

    dt = _softplus(dt_ref[...] + dtb_ref[...])
    dta = dt * (-jnp.exp(alog_ref[...]))
    cs = _dot_exact_rhs(_lower_tri(L, BF16), dta)

    def expand(width):
        r = lax.broadcasted_iota(jnp.int32, (LANES, R * width), 0)
        c = lax.broadcasted_iota(jnp.int32, (LANES, R * width), 1)
        return (r == g * R + c // width).astype(BF16)

    sel_p = expand(P)
    dt_x = _dot_exact_lhs(dt, sel_p)
    cs_x = _dot_exact_lhs(cs, sel_p)
    cs_w = _dot_exact_lhs(cs, expand(L))

    xdt = xc * dt_x
    cs_last = cs_x[L - 1:L, :]
    st = st_ref[...]
    y = _dot(cc16, st.astype(BF16)) * jnp.exp(cs_x)
    st_ref[...] = st * jnp.exp(cs_last) + _dot_tn(
        bc16, (xdt * jnp.exp(cs_last - cs_x)).astype(BF16))

    cb = _dot_nt(cc16, bc16)
    rr = lax.broadcasted_iota(jnp.int32, (L, L), 0)
    cl = lax.broadcasted_iota(jnp.int32, (L, L), 1)
    causal = rr >= cl
    xdt16 = xdt.astype(BF16)
    parts = []
    for r in range(R):
        col = cs_w[:, r * L:(r + 1) * L]
        seg = col - col.T
        decay = jnp.where(causal, jnp.exp(jnp.where(causal, seg, 0.0)), 0.0)
        parts.append(_dot((cb * decay).astype(BF16), xdt16[:, r * P:(r + 1) * P]))
    y = y + jnp.concatenate(parts, axis=1)

    y = y + dskip_ref[...] * xc
    yz = y * _silu(z_ref[...].astype(F32))
    o_ref[...] = _rms_rows(yz, norm_ref[...]).astype(o_ref.dtype)


def ssd_mix(pb, dt_raw, conv_w, conv_b, dt_bias, a_log, d_skip, b_norm,
            batch, seq, d_model):
    L, G, N = SSD_CHUNK, SSD_GROUPS, SSD_STATE
    n_heads = d_model // SSD_HEAD_DIM
    R = n_heads // G
    gw = d_model // G
    steps = seq // L
    m = batch * seq
    halo = 8
    bn0 = 5 * d_model // N
    cwb0 = d_model // N

    def pad_heads(v):
        return jnp.pad(v.astype(F32), (0, LANES - n_heads)).reshape(1, LANES)

    def rep_heads(v):
        return jnp.repeat(v.astype(F32), SSD_HEAD_DIM).reshape(1, d_model)

    row_g = lambda off: pl.BlockSpec((L, gw), lambda b, g, s, off=off: (b * steps + s, off + g))
    row_n = lambda off: pl.BlockSpec((L, N), lambda b, g, s, off=off: (b * steps + s, off + g))
    par_g = lambda rows: pl.BlockSpec((rows, gw), lambda b, g, s: (0, g))
    par_n = lambda rows, off: pl.BlockSpec((rows, N), lambda b, g, s, off=off: (0, off + g))
    lane_vec = pl.BlockSpec((1, LANES), lambda b, g, s: (0, 0))
    cb2 = conv_b.reshape(1, -1)

    return pl.pallas_call(
        functools.partial(_ssd_kernel, heads_per_group=R),
        grid=(batch, G, steps),
        in_specs=[row_g(4 * G), row_n(bn0), row_n(bn0 + G), row_g(3 * G),
                  pl.BlockSpec((L, LANES), lambda b, g, s: (b * steps + s, 0)),
                  par_g(SSD_CONV), par_n(SSD_CONV, cwb0), par_n(SSD_CONV, cwb0 + G),
                  par_g(1), par_n(1, cwb0), par_n(1, cwb0 + G),
                  lane_vec, lane_vec, par_g(1), par_g(1)],
        out_specs=pl.BlockSpec((L, gw), lambda b, g, s: (b * steps + s, g)),
        out_shape=jax.ShapeDtypeStruct((m, d_model), BF16),
        scratch_shapes=[pltpu.VMEM((N, gw), F32),
                        pltpu.VMEM((halo, gw), F32), pltpu.VMEM((halo, N), F32),
                        pltpu.VMEM((halo, N), F32),
                        pltpu.VMEM((halo + L, gw), F32), pltpu.VMEM((halo + L, N), F32),
                        pltpu.VMEM((halo + L, N), F32)],
        compiler_params=_cparams(("arbitrary", "arbitrary", "arbitrary")),
        name="ssd",
    )(pb, pb, pb, pb, dt_raw,
      conv_w, conv_w, conv_w, cb2, cb2, cb2,
      pad_heads(dt_bias), pad_heads(a_log), rep_heads(d_skip), b_norm.reshape(1, d_model))


def _conf_tail_kernel(c_ref, dww_ref, dwb_ref, lng_ref, lnb_ref, w2_ref, b2_ref,
                      h_ref, gpost_ref, gnext_ref, hout_ref, unext_ref,
                      carry_ref, ext_ref, conv_ref, *, row_block):
    tm, d = c_ref.shape
    halo = CONF_HALO
    first = halo - (CONF_KERNEL - 1)
    n_tiles = d // LANES

    @pl.when(pl.program_id(1) == 0)
    def _():
        carry_ref[...] = jnp.zeros_like(carry_ref)

    for j in range(n_tiles):
        ln = slice(j * LANES, (j + 1) * LANES)
        ext_ref[j, 0:halo, :] = carry_ref[j]
        ext_ref[j, halo:halo + tm, :] = c_ref[:, ln]
        carry_ref[j] = c_ref[tm - halo:tm, ln]

    def col_tile(j, carry):
        cj = pl.ds(pl.multiple_of(j * LANES, LANES), LANES)
        for rb in range(tm // row_block):
            base = rb * row_block
            acc = dwb_ref[:, cj] + dww_ref[0:1, cj] * ext_ref[j, pl.ds(base + first, row_block), :]
            for k in range(1, CONF_KERNEL):
                acc = acc + dww_ref[k:k + 1, cj] * ext_ref[j, pl.ds(base + first + k, row_block), :]
            conv_ref[base:base + row_block, cj] = acc
        return carry

    lax.fori_loop(0, n_tiles, col_tile, 0)

    x = conv_ref[...]
    mu = jnp.mean(x, axis=-1, keepdims=True)
    xc = x - mu
    var = jnp.mean(xc * xc, axis=-1, keepdims=True)
    y = _silu(xc * lax.rsqrt(var + LN_EPS) * lng_ref[...] + lnb_ref[...])
    m = _dot(y.astype(BF16), w2_ref[...]) + b2_ref[...]
    _residual_epilogue(m, h_ref[...], gpost_ref[...], gnext_ref[...], hout_ref, unext_ref)


def conf_tail(c, dw_w, dw_b, ln_g, ln_b, w2_stack_bf16, layer, b2, h, gpost, gnext,
              batch, seq, tm=256, row_block=64):
    m, d = c.shape
    tm = min(tm, seq)
    row_block = min(row_block, tm)
    steps = seq // tm
    row = pl.BlockSpec((tm, d), lambda b, s: (b * steps + s, 0))
    vec = pl.BlockSpec((1, d), lambda b, s: (0, 0))
    dww = jnp.pad(dw_w, ((0, CONF_HALO - CONF_KERNEL), (0, 0)))
    return pl.pallas_call(
        functools.partial(_conf_tail_kernel, row_block=row_block),
        grid=(batch, steps),
        in_specs=[row, pl.BlockSpec((CONF_HALO, d), lambda b, s: (0, 0)), vec, vec, vec,
                  pl.BlockSpec((None, d, d), lambda b, s: (layer, 0, 0),
                               pipeline_mode=pl.Buffered(1)),
                  vec, row, vec, vec],
        out_specs=[row, row],
        out_shape=[jax.ShapeDtypeStruct((m, d), F32), jax.ShapeDtypeStruct((m, d), BF16)],
        scratch_shapes=[pltpu.VMEM((d // LANES, CONF_HALO, LANES), F32),
                        pltpu.VMEM((d // LANES, CONF_HALO + tm, LANES), F32),
                        pltpu.VMEM((tm, d), F32)],
        compiler_params=_cparams(("arbitrary", "arbitrary")),
        name="conf_tail",
    )(c, dww, dw_b.reshape(1, d), ln_g.reshape(1, d), ln_b.reshape(1, d), w2_stack_bf16,
      b2.reshape(1, d), h, gpost.reshape(1, d), gnext.reshape(1, d))


def kernel(x, mix_pre_g, mix_post_g, ffn_pre_g, ffn_post_g, hgrn_lb_logits, even_w_in,
           hgrn_norm_g, ssd_conv_w, ssd_conv_b, ssd_dt_bias, ssd_a_log, ssd_d, ssd_norm_g,
           even_w_out, conf_w1, conf_b1, conf_dw_w, conf_dw_b, conf_ln_g, conf_ln_b,
           conf_w2, conf_b2, ffn_w_gate, ffn_w_up, ffn_w_down):
    batch, seq, d = x.shape
    depth = mix_pre_g.shape[0]
    hidden = ffn_w_gate.shape[2]
    m = batch * seq
    tn = 512
    tn_in = min(1024, d)
    main_cols = 6 * d + 2 * SSD_GROUPS * SSD_STATE
    n_ssd_heads = d // SSD_HEAD_DIM
    f_blocks = d // tn_in

    lb_p = jax.nn.softmax(hgrn_lb_logits.astype(F32), axis=0)
    lower_bounds = jnp.cumsum(lb_p, axis=0) - lb_p[0]

    w_out16 = even_w_out.astype(BF16)
    w_down16 = ffn_w_down.astype(BF16)
    w2_16 = conf_w2.astype(BF16)
    conf_b1_3d = conf_b1.reshape(conf_b1.shape[0], 1, -1)
    plus = lambda off: (lambda n: n + off)

    h = x.reshape(m, d)
    u = prenorm(h, mix_pre_g[0])
    for layer in range(depth):
        i = layer // 2
        if layer % 2 == 0:
            skip_f = lambda n: n + jnp.where(n >= f_blocks, f_blocks, 0)
            pb = wide_proj(u, [even_w_in], i, [skip_f], [], [], (main_cols - d) // tn_in,
                           _identity_epilogue, BF16, "even_in_proj", tn=tn_in)
            f_pre = wide_proj(u, [even_w_in], i, [plus(f_blocks)], [], [], f_blocks,
                              _identity_epilogue, F32, "even_f_proj", tn=tn_in)
            w_dt = jnp.pad(even_w_in[i, :, main_cols:],
                           ((0, 0), (0, LANES - n_ssd_heads))).astype(BF16)
            dt_raw = small_proj(u, w_dt)
            o_a = hgrn_mix(pb, f_pre, lower_bounds[i], hgrn_norm_g[i], batch, seq, d)
            o_b = ssd_mix(pb, dt_raw, ssd_conv_w[i], ssd_conv_b[i], ssd_dt_bias[i],
                          ssd_a_log[i], ssd_d[i], ssd_norm_g[i], batch, seq, d)
            h, u = out_proj([o_a, o_b], w_out16, i, h, mix_post_g[layer],
                            ffn_pre_g[layer], True, "even_out_proj")
        else:
            c = wide_proj(u, [conf_w1, conf_w1], i, [plus(0), plus(d // tn)],
                          [conf_b1_3d, conf_b1_3d], [plus(0), plus(d // tn)], d // tn,
                          _glu_epilogue, F32, "conf_glu", tn=tn)
            h, u = conf_tail(c, conf_dw_w[i], conf_dw_b[i], conf_ln_g[i], conf_ln_b[i],
                             w2_16, i, conf_b2[i], h, mix_post_g[layer],
                             ffn_pre_g[layer], batch, seq)
        act = wide_proj(u, [ffn_w_gate, ffn_w_up], layer, [plus(0), plus(0)], [], [],
                        hidden // tn, _swiglu_epilogue, BF16, "ffn_in", tn=tn)
        last = layer == depth - 1
        gnext = mix_pre_g[layer] if last else mix_pre_g[layer + 1]
        h, u = out_proj([act], w_down16, layer, h, ffn_post_g[layer], gnext, not last,
                        "ffn_out")
    return h.reshape(batch, seq, d)
```

```python
import functools

import jax
import jax.numpy as jnp
from jax import lax
from jax.experimental import pallas as pl
from jax.experimental.pallas import tpu as pltpu

F32 = jnp.float32
BF16 = jnp.bfloat16

RMS_EPS = 1e-6
LN_EPS = 1e-5
HGRN_F_MIN = 1e-6
HGRN_HEAD = 128
HGRN_CHUNK = 64
HGRN_SUB = 8
HGRN_HEADS_PER_PART = 4
HGRN_PARTS_PER_STEP = 4
SSD_HEAD_DIM = 64
SSD_GROUPS = 4
SSD_STATE = 128
SSD_CONV = 4
SSD_CHUNK = 128
SSD_GROUPS_PER_STEP = 4
CONF_KERNEL = 31
CONF_HALO = 32
LANES = 128
VMEM_LIMIT = 56 * 1024 * 1024


def _cparams(semantics):
    return pltpu.CompilerParams(dimension_semantics=semantics,
                                vmem_limit_bytes=VMEM_LIMIT)


def _sigmoid(x):
    return 0.5 * jnp.tanh(0.5 * x) + 0.5


def _silu(x):
    t = 0.5 * x
    return t * jnp.tanh(t) + t


def _softplus(x):
    return jnp.maximum(x, 0.0) + jnp.log1p(jnp.exp(-jnp.abs(x)))


def _rms_rows(x, g, eps=RMS_EPS):
    ms = jnp.mean(x * x, axis=-1, keepdims=True)
    return x * lax.rsqrt(ms + eps) * g


def _dot(a, b):
    return jnp.dot(a, b, preferred_element_type=F32)


def _dot_nt(a, b):
    return lax.dot_general(a, b, (((1,), (1,)), ((), ())), preferred_element_type=F32)


def _dot_tn(a, b):
    return lax.dot_general(a, b, (((0,), (0,)), ((), ())), preferred_element_type=F32)


def _split3(x):
    hi = x.astype(BF16)
    r1 = x - hi.astype(F32)
    mid = r1.astype(BF16)
    lo = (r1 - mid.astype(F32)).astype(BF16)
    return hi, mid, lo


def _dot_exact_rhs(sel, x):
    return _dot(jnp.concatenate([sel, sel, sel], axis=1),
                jnp.concatenate(_split3(x), axis=0))


def _lane_stack3(x, n):
    hi, mid, lo = _split3(x)
    return (hi.astype(F32) + pltpu.roll(mid.astype(F32), n, 1)
            + pltpu.roll(lo.astype(F32), 2 * n, 1)).astype(BF16)


def _lower_tri(n, dtype):
    r = lax.broadcasted_iota(jnp.int32, (n, n), 0)
    c = lax.broadcasted_iota(jnp.int32, (n, n), 1)
    return (r >= c).astype(dtype)


def _prenorm_kernel(x_ref, g_ref, o_ref):
    o_ref[...] = _rms_rows(x_ref[...], g_ref[...]).astype(o_ref.dtype)


def prenorm(x, g, tm=512):
    m, d = x.shape
    tm = min(tm, m)
    return pl.pallas_call(
        _prenorm_kernel,
        grid=(m // tm,),
        in_specs=[pl.BlockSpec((tm, d), lambda i: (i, 0)),
                  pl.BlockSpec((1, d), lambda i: (0, 0))],
        out_specs=pl.BlockSpec((tm, d), lambda i: (i, 0)),
        out_shape=jax.ShapeDtypeStruct((m, d), BF16),
        compiler_params=_cparams(("arbitrary",)),
        name="prenorm",
    )(x, g.reshape(1, d))


def _wide_kernel(*refs, n_w, n_b, epilogue, w_is_transposed):
    u_ref = refs[0]
    w_refs = refs[1:1 + n_w]
    b_refs = refs[1 + n_w:1 + n_w + n_b]
    o_ref = refs[1 + n_w + n_b]
    s_refs = refs[2 + n_w + n_b:]

    @pl.when(pl.program_id(1) == 0)
    def _():
        for w, s in zip(w_refs, s_refs):
            s[...] = w[...].astype(BF16)

    u = u_ref[...]
    dot = _dot_nt if w_is_transposed else _dot
    ys = [dot(u, s[...]) for s in s_refs]
    o_ref[...] = epilogue(*ys, *[b[...] for b in b_refs]).astype(o_ref.dtype)


def wide_proj(u, ws, layer, w_col_maps, bs, b_col_maps, n_blocks, epilogue, out_dtype,
              name, tn=512, tm=1024, w_is_transposed=False):
    m, k = u.shape
    tm = min(tm, m)
    in_specs = [pl.BlockSpec((tm, k), lambda n, i: (i, 0))]
    for cmap in w_col_maps:
        if w_is_transposed:
            in_specs.append(pl.BlockSpec((None, tn, k), lambda n, i, cmap=cmap: (layer, cmap(n), 0)))
        else:
            in_specs.append(pl.BlockSpec((None, k, tn), lambda n, i, cmap=cmap: (layer, 0, cmap(n))))
    for cmap in b_col_maps:
        in_specs.append(pl.BlockSpec((None, 1, tn), lambda n, i, cmap=cmap: (layer, 0, cmap(n))))
    return pl.pallas_call(
        functools.partial(_wide_kernel, n_w=len(ws), n_b=len(bs), epilogue=epilogue,
                          w_is_transposed=w_is_transposed),
        grid=(n_blocks, m // tm),
        in_specs=in_specs,
        out_specs=pl.BlockSpec((tm, tn), lambda n, i: (i, n)),
        out_shape=jax.ShapeDtypeStruct((m, n_blocks * tn), out_dtype),
        scratch_shapes=[pltpu.VMEM((tn, k) if w_is_transposed else (k, tn), BF16) for _ in ws],
        compiler_params=_cparams(("arbitrary", "arbitrary")),
        name=name,
    )(u, *ws, *bs)


def _identity_epilogue(y):
    return y


def _swiglu_epilogue(g, up):
    return _silu(g) * up


def _glu_epilogue(a, g, ba, bg):
    return (a + ba) * _sigmoid(g + bg)


def _small_proj_kernel(u_ref, w_ref, o_ref):
    n, k = w_ref.shape
    w = jnp.concatenate([w_ref[...], jnp.zeros((LANES - n, k), F32)], axis=0)
    o_ref[...] = _dot_nt(u_ref[...], w.astype(BF16))


def small_proj(u, w_t_stack, layer, first_row, n, tm=1024):
    m, k = u.shape
    tm = min(tm, m)
    return pl.pallas_call(
        _small_proj_kernel,
        grid=(m // tm,),
        in_specs=[pl.BlockSpec((tm, k), lambda i: (i, 0)),
                  pl.BlockSpec((None, n, k), lambda i: (layer, first_row // n, 0))],
        out_specs=pl.BlockSpec((tm, LANES), lambda i: (i, 0)),
        out_shape=jax.ShapeDtypeStruct((m, LANES), F32),
        compiler_params=_cparams(("arbitrary",)),
        name="dt_proj",
    )(u, w_t_stack)


def _residual_epilogue(m, h, gpost, gnext, hout_ref, unext_ref):
    hn = h + _rms_rows(m, gpost)
    hout_ref[...] = hn
    if unext_ref is not None:
        unext_ref[...] = _rms_rows(hn, gnext).astype(unext_ref.dtype)


def _out_proj_kernel(*refs, n_a, with_next):
    a_refs = refs[:n_a]
    w_ref, h_ref, gpost_ref, gnext_ref, hout_ref = refs[n_a:n_a + 5]
    unext_ref = refs[n_a + 5] if with_next else None
    off = 0
    m = None
    for a_ref in a_refs:
        ka = a_ref.shape[1]
        part = _dot(a_ref[...], w_ref[off:off + ka, :])
        m = part if m is None else m + part
        off += ka
    _residual_epilogue(m, h_ref[...], gpost_ref[...], gnext_ref[...], hout_ref, unext_ref)


def out_proj(acts, w_stack_bf16, layer, h, gpost, gnext, with_next, name, tm=256):
    m = h.shape[0]
    _, kk, d = w_stack_bf16.shape
    tm = min(tm, m)
    row = pl.BlockSpec((tm, d), lambda i: (i, 0))
    vec = pl.BlockSpec((1, d), lambda i: (0, 0))
    out_shape = [jax.ShapeDtypeStruct((m, d), F32)]
    out_specs = [row]
    if with_next:
        out_shape.append(jax.ShapeDtypeStruct((m, d), BF16))
        out_specs.append(row)
    res = pl.pallas_call(
        functools.partial(_out_proj_kernel, n_a=len(acts), with_next=with_next),
        grid=(m // tm,),
        in_specs=[pl.BlockSpec((tm, a.shape[1]), lambda i: (i, 0)) for a in acts] + [
            pl.BlockSpec((None, kk, d), lambda i: (layer, 0, 0),
                         pipeline_mode=pl.Buffered(1)),
            row, vec, vec],
        out_specs=out_specs,
        out_shape=out_shape,
        compiler_params=_cparams(("arbitrary",)),
        name=name,
    )(*acts, w_stack_bf16, h, gpost.reshape(1, d), gnext.reshape(1, d))
    return (res[0], res[1]) if with_next else (res[0], None)


def _hgrn_kernel(q_ref, f_ref, v_ref, gate_ref, lb_ref, an_ref, o_ref, st_ref,
                 *, n_chunks, n_heads, part_heads):
    C, SUB, HD = HGRN_CHUNK, HGRN_SUB, HGRN_HEAD
    n_sub = C // SUB
    mid = SUB // 2 - 1
    width = part_heads * HD
    head_lanes = [slice(h * HD, (h + 1) * HD) for h in range(part_heads)]
    parts = [slice(p * width, (p + 1) * width) for p in range(n_heads // part_heads)]

    @pl.when(pl.program_id(2) == 0)
    def _():
        st_ref[...] = jnp.zeros_like(st_ref)

    tri = _lower_tri(C, BF16)
    causal = (lax.broadcasted_iota(jnp.int32, (C, C), 0)
              >= lax.broadcasted_iota(jnp.int32, (C, C), 1))
    def gates(sl, part):
        lb = lb_ref[:, part]
        sig = _sigmoid(f_ref[sl, part])
        f = lb + (1.0 - lb) * sig
        k = (1.0 - lb) * (1.0 - sig)
        log_f = jnp.log(jnp.maximum(f, HGRN_F_MIN))
        qs = _silu(q_ref[sl, part].astype(F32))
        b = _dot_exact_rhs(tri, log_f)
        return qs, k, b, v_ref[sl, part]

    def scores_and_state(vals, h0):
        qs, k, b, v16 = vals
        b_last = b[C - 1:C, :]

        q_in16 = (qs * jnp.exp(b)).astype(BF16)
        k_end16 = (k * jnp.exp(b_last - b)).astype(BF16)
        dec_last = jnp.exp(b_last)
        sts = [st_ref[h0 + h] for h in range(part_heads)]
        o_heads = [_dot_nt(q_in16[:, s], st.astype(BF16)) for s, st in zip(head_lanes, sts)]
        for h, (s, st) in enumerate(zip(head_lanes, sts)):
            st_ref[h0 + h] = st * dec_last[:, s] + _dot_tn(v16[:, s], k_end16[:, s])

        blocks = [[] for _ in range(part_heads)]
        keys = []
        prev_ref = None
        for i in range(n_sub):
            rows = slice(i * SUB, (i + 1) * SUB)
            ref_b = b[i * SUB + mid:i * SUB + mid + 1, :]
            if keys:
                step = jnp.exp(ref_b - prev_ref)
                keys = [kj * step for kj in keys]
            keys.append(k[rows, :] * jnp.exp(ref_b - b[rows, :]))
            prev_ref = ref_b
            qt16 = (qs[rows, :] * jnp.exp(b[rows, :] - ref_b)).astype(BF16)
            pad = [jnp.zeros(((n_sub - 1 - i) * SUB, width), F32)] if i < n_sub - 1 else []
            kt16 = jnp.concatenate(keys + pad, axis=0).astype(BF16)
            for h, s in enumerate(head_lanes):
                blocks[h].append(_dot_nt(qt16[:, s], kt16[:, s]))
        return o_heads, blocks

    def outputs(sl, part, vals, o_heads, blocks):
        v16 = vals[3]
        for h, s in enumerate(head_lanes):
            scores = jnp.where(causal, jnp.concatenate(blocks[h], axis=0), 0.0)
            o_heads[h] = o_heads[h] + _dot(scores.astype(BF16), v16[:, s])
        y = jnp.concatenate(
            [o * lax.rsqrt(jnp.mean(o * o, axis=-1, keepdims=True) + RMS_EPS) for o in o_heads],
            axis=1)
        y = y * an_ref[:, part] * _silu(gate_ref[sl, part].astype(F32))
        o_ref[sl, part] = y.astype(o_ref.dtype)

    def chunk(c, carry):
        sl = pl.ds(pl.multiple_of(c * C, C), C)
        vals = [gates(sl, part) for part in parts]
        mids = [scores_and_state(v, p * part_heads) for p, v in enumerate(vals)]
        for part, v, (o_heads, blocks) in zip(parts, vals, mids):
            outputs(sl, part, v, o_heads, blocks)
        return carry

    lax.fori_loop(0, n_chunks, chunk, 0)


def hgrn_mix(pb, f_pre, lb, a_norm, batch, seq, d_model, rows_per_step=512):
    n_heads = d_model // HGRN_HEAD
    part_heads = min(HGRN_HEADS_PER_PART, n_heads)
    hb = min(HGRN_PARTS_PER_STEP * part_heads, n_heads)
    gw = hb * HGRN_HEAD
    n_groups = n_heads // hb
    cs = min(rows_per_step, seq)
    steps = seq // cs
    m = batch * seq

    def col(seg):
        return pl.BlockSpec((cs, gw), lambda b, h, s, seg=seg: (b * steps + s, seg * n_groups + h))

    vec = pl.BlockSpec((1, gw), lambda b, h, s: (0, h))
    return pl.pallas_call(
        functools.partial(_hgrn_kernel, n_chunks=cs // HGRN_CHUNK, n_heads=hb,
                          part_heads=part_heads),
        grid=(batch, n_groups, steps),
        in_specs=[col(0), col(0), col(1), col(2), vec, vec],
        out_specs=col(0),
        out_shape=jax.ShapeDtypeStruct((m, d_model), BF16),
        scratch_shapes=[pltpu.VMEM((hb, HGRN_HEAD, HGRN_HEAD), F32)],
        compiler_params=_cparams(("arbitrary", "arbitrary", "arbitrary")),
        name="hgrn2",
    )(pb, f_pre, pb, pb, lb.reshape(1, d_model), a_norm.reshape(1, d_model))


def _causal_conv(raw_ref, carry_ref, ext_ref, w_ref, b_ref, n_rows, n_taps, halo):
    first = halo - n_taps + 1
    outs = []
    for j in range(raw_ref.shape[1] // LANES):
        ln = slice(j * LANES, (j + 1) * LANES)
        raw = raw_ref[:, ln].astype(F32)
        ext_ref[j, 0:halo, :] = carry_ref[j]
        ext_ref[j, halo:halo + n_rows, :] = raw
        carry_ref[j] = raw[n_rows - halo:n_rows, :]
        acc = b_ref[:, ln] + w_ref[0:1, ln] * ext_ref[j, pl.ds(first, n_rows), :]
        for k in range(1, n_taps):
            acc = acc + w_ref[k:k + 1, ln] * ext_ref[j, pl.ds(first + k, n_rows), :]
        outs.append(acc)
    return jnp.concatenate(outs, axis=1)


def _ssd_kernel(x_ref, bm_ref, cm_ref, z_ref, dt_ref,
                cwx_ref, cwb_ref, cwc_ref, cbx_ref, cbb_ref, cbc_ref,
                dtb_ref, alog_ref, selp_ref, selw_ref, dskip_ref, norm_ref, o_ref,
                st_ref, carx_ref, carb_ref, carc_ref, extx_ref, extb_ref, extc_ref,
                *, heads_per_group, n_groups, n_heads):
    L, P, R, N = SSD_CHUNK, SSD_HEAD_DIM, heads_per_group, SSD_STATE
    gw = R * P
    groups = range(n_groups)
    g_lanes = [slice(j * gw, (j + 1) * gw) for j in groups]
    n_lanes = [slice(j * N, (j + 1) * N) for j in groups]

    @pl.when(pl.program_id(2) == 0)
    def _():
        st_ref[...] = jnp.zeros_like(st_ref)
        carx_ref[...] = jnp.zeros_like(carx_ref)
        carb_ref[...] = jnp.zeros_like(carb_ref)
        carc_ref[...] = jnp.zeros_like(carc_ref)

    halo = carx_ref.shape[1]
    xc = _silu(_causal_conv(x_ref, carx_ref, extx_ref, cwx_ref, cbx_ref, L, SSD_CONV, halo))
    bc16 = _silu(_causal_conv(bm_ref, carb_ref, extb_ref, cwb_ref, cbb_ref, L, SSD_CONV,
                              halo)).astype(BF16)
    cc16 = _silu(_causal_conv(cm_ref, carc_ref, extc_ref, cwc_ref, cbc_ref, L, SSD_CONV,
                              halo)).astype(BF16)

    lane = lax.broadcasted_iota(jnp.int32, (L, LANES), 1)
    dt = jnp.where(lane < n_heads, _softplus(dt_ref[...] + dtb_ref[...]), 0.0)
    dta = dt * (-jnp.exp(alog_ref[...]))
    cs = _dot_exact_rhs(_lower_tri(L, BF16), dta)

    dt3 = _lane_stack3(dt, n_heads)
    cs3 = _lane_stack3(cs, n_heads)
    dt_x = [_dot(dt3, selp_ref[j]) for j in groups]
    cs_x = [_dot(cs3, selp_ref[j]) for j in groups]
    cs_w = [_dot(cs3, selw_ref[j]) for j in groups]

    rr = lax.broadcasted_iota(jnp.int32, (L, L), 0)
    cl = lax.broadcasted_iota(jnp.int32, (L, L), 1)
    causal = rr >= cl
    ys, xdt16s, cbs = [], [], []
    for j in groups:
        xdt = xc[:, g_lanes[j]] * dt_x[j]
        cs_last = cs_x[j][L - 1:L, :]
        st = st_ref[j]
        ys.append(_dot(cc16[:, n_lanes[j]], st.astype(BF16)) * jnp.exp(cs_x[j]))
        st_ref[j] = st * jnp.exp(cs_last) + _dot_tn(
            bc16[:, n_lanes[j]], (xdt * jnp.exp(cs_last - cs_x[j])).astype(BF16))
        cbs.append(_dot_nt(cc16[:, n_lanes[j]], bc16[:, n_lanes[j]]))
        xdt16s.append(xdt.astype(BF16))

    parts = [[] for _ in groups]
    for r in range(R):
        for j in groups:
            col = cs_w[j][:, r * L:(r + 1) * L]
            seg = col - col.T
            decay = jnp.where(causal, jnp.exp(jnp.where(causal, seg, 0.0)), 0.0)
            parts[j].append(_dot((cbs[j] * decay).astype(BF16),
                                 xdt16s[j][:, r * P:(r + 1) * P]))

    for j in groups:
        ln = g_lanes[j]
        y = ys[j] + jnp.concatenate(parts[j], axis=1) + dskip_ref[:, ln] * xc[:, ln]
        yz = y * _silu(z_ref[:, ln].astype(F32))
        o_ref[:, ln] = _rms_rows(yz, norm_ref[:, ln]).astype(o_ref.dtype)


def ssd_mix(pb, dt_raw, conv_w, conv_b, dt_bias, a_log, d_skip, b_norm,
            batch, seq, d_model):
    L, G, N = SSD_CHUNK, SSD_GROUPS, SSD_STATE
    n_heads = d_model // SSD_HEAD_DIM
    R = n_heads // G
    gw = d_model // G
    steps = seq // L
    m = batch * seq
    halo = 8
    gs = SSD_GROUPS_PER_STEP
    gsw, gsn = gs * gw, gs * N
    bn0 = 5 * d_model // gsn
    cwb0 = d_model // gsn

    def rep_heads(v):
        return jnp.repeat(v.astype(F32), SSD_HEAD_DIM).reshape(1, d_model)

    def pad_heads(v):
        return jnp.pad(v.astype(F32), (0, LANES - n_heads)).reshape(1, LANES)

    def select(width):
        row = jnp.arange(LANES, dtype=jnp.int32)[None, :, None]
        lane = jnp.arange(R * width, dtype=jnp.int32)[None, None, :]
        grp = jnp.arange(G, dtype=jnp.int32)[:, None, None]
        hit = (row % n_heads == grp * R + lane // width) & (row < 3 * n_heads)
        return hit.astype(BF16)

    row_g = lambda off: pl.BlockSpec((L, gsw), lambda b, g, s, off=off: (b * steps + s, off + g))
    row_n = lambda off: pl.BlockSpec((L, gsn), lambda b, g, s, off=off: (b * steps + s, off + g))
    par_g = lambda rows: pl.BlockSpec((rows, gsw), lambda b, g, s: (0, g))
    par_n = lambda rows, off: pl.BlockSpec((rows, gsn), lambda b, g, s, off=off: (0, off + g))
    head_vec = pl.BlockSpec((1, LANES), lambda b, g, s: (0, 0))
    sel = lambda width: pl.BlockSpec((gs, LANES, R * width), lambda b, g, s: (g, 0, 0))
    cb2 = conv_b.reshape(1, -1)
    n_steps_g = G // gs

    return pl.pallas_call(
        functools.partial(_ssd_kernel, heads_per_group=R, n_groups=gs, n_heads=n_heads),
        grid=(batch, n_steps_g, steps),
        in_specs=[row_g(4 * n_steps_g), row_n(bn0), row_n(bn0 + n_steps_g), row_g(3 * n_steps_g),
                  pl.BlockSpec((L, LANES), lambda b, g, s: (b * steps + s, 0)),
                  par_g(SSD_CONV), par_n(SSD_CONV, cwb0), par_n(SSD_CONV, cwb0 + n_steps_g),
                  par_g(1), par_n(1, cwb0), par_n(1, cwb0 + n_steps_g),
                  head_vec, head_vec, sel(SSD_HEAD_DIM), sel(L), par_g(1), par_g(1)],
        out_specs=pl.BlockSpec((L, gsw), lambda b, g, s: (b * steps + s, g)),
        out_shape=jax.ShapeDtypeStruct((m, d_model), BF16),
        scratch_shapes=[pltpu.VMEM((gs, N, gw), F32),
                        pltpu.VMEM((gsw // LANES, halo, LANES), F32),
                        pltpu.VMEM((gsn // LANES, halo, LANES), F32),
                        pltpu.VMEM((gsn // LANES, halo, LANES), F32),
                        pltpu.VMEM((gsw // LANES, halo + L, LANES), F32),
                        pltpu.VMEM((gsn // LANES, halo + L, LANES), F32),
                        pltpu.VMEM((gsn // LANES, halo + L, LANES), F32)],
        compiler_params=_cparams(("arbitrary", "arbitrary", "arbitrary")),
        name="ssd",
    )(pb, pb, pb, pb, dt_raw,
      conv_w, conv_w, conv_w, cb2, cb2, cb2,
      pad_heads(dt_bias), pad_heads(a_log),
      select(SSD_HEAD_DIM), select(L), rep_heads(d_skip), b_norm.reshape(1, d_model))


def _conf_tail_kernel(c_ref, dww_ref, dwb_ref, lng_ref, lnb_ref, w2_ref, b2_ref,
                      h_ref, gpost_ref, gnext_ref, hout_ref, unext_ref,
                      carry_ref, ext_ref, conv_ref, *, row_block):
    tm, d = c_ref.shape
    halo = CONF_HALO
    first = halo - (CONF_KERNEL - 1)
    n_tiles = d // LANES

    @pl.when(pl.program_id(1) == 0)
    def _():
        carry_ref[...] = jnp.zeros_like(carry_ref)

    for j in range(n_tiles):
        ln = slice(j * LANES, (j + 1) * LANES)
        ext_ref[j, 0:halo, :] = carry_ref[j]
        ext_ref[j, halo:halo + tm, :] = c_ref[:, ln]
        carry_ref[j] = c_ref[tm - halo:tm, ln]

    def col_tile(j, carry):
        cj = pl.ds(pl.multiple_of(j * LANES, LANES), LANES)
        for rb in range(tm // row_block):
            base = rb * row_block
            acc = dwb_ref[:, cj] + dww_ref[0:1, cj] * ext_ref[j, pl.ds(base + first, row_block), :]
            for k in range(1, CONF_KERNEL):
                acc = acc + dww_ref[k:k + 1, cj] * ext_ref[j, pl.ds(base + first + k, row_block), :]
            conv_ref[base:base + row_block, cj] = acc
        return carry

    lax.fori_loop(0, n_tiles, col_tile, 0)

    x = conv_ref[...]
    mu = jnp.mean(x, axis=-1, keepdims=True)
    xc = x - mu
    var = jnp.mean(xc * xc, axis=-1, keepdims=True)
    y = _silu(xc * lax.rsqrt(var + LN_EPS) * lng_ref[...] + lnb_ref[...])
    m = _dot(y.astype(BF16), w2_ref[...]) + b2_ref[...]
    _residual_epilogue(m, h_ref[...], gpost_ref[...], gnext_ref[...], hout_ref, unext_ref)


def conf_tail(c, dw_w, dw_b, ln_g, ln_b, w2_stack_bf16, layer, b2, h, gpost, gnext,
              batch, seq, tm=256, row_block=64):
    m, d = c.shape
    tm = min(tm, seq)
    row_block = min(row_block, tm)
    steps = seq // tm
    row = pl.BlockSpec((tm, d), lambda b, s: (b * steps + s, 0))
    vec = pl.BlockSpec((1, d), lambda b, s: (0, 0))
    dww = jnp.pad(dw_w, ((0, CONF_HALO - CONF_KERNEL), (0, 0)))
    return pl.pallas_call(
        functools.partial(_conf_tail_kernel, row_block=row_block),
        grid=(batch, steps),
        in_specs=[row, pl.BlockSpec((CONF_HALO, d), lambda b, s: (0, 0)), vec, vec, vec,
                  pl.BlockSpec((None, d, d), lambda b, s: (layer, 0, 0),
                               pipeline_mode=pl.Buffered(1)),
                  vec, row, vec, vec],
        out_specs=[row, row],
        out_shape=[jax.ShapeDtypeStruct((m, d), F32), jax.ShapeDtypeStruct((m, d), BF16)],
        scratch_shapes=[pltpu.VMEM((d // LANES, CONF_HALO, LANES), F32),
                        pltpu.VMEM((d // LANES, CONF_HALO + tm, LANES), F32),
                        pltpu.VMEM((tm, d), F32)],
        compiler_params=_cparams(("arbitrary", "arbitrary")),
        name="conf_tail",
    )(c, dww, dw_b.reshape(1, d), ln_g.reshape(1, d), ln_b.reshape(1, d), w2_stack_bf16,
      b2.reshape(1, d), h, gpost.reshape(1, d), gnext.reshape(1, d))


def kernel(x, mix_pre_g, mix_post_g, ffn_pre_g, ffn_post_g, hgrn_lb_logits, even_w_in,
           hgrn_norm_g, ssd_conv_w, ssd_conv_b, ssd_dt_bias, ssd_a_log, ssd_d, ssd_norm_g,
           even_w_out, conf_w1, conf_b1, conf_dw_w, conf_dw_b, conf_ln_g, conf_ln_b,
           conf_w2, conf_b2, ffn_w_gate, ffn_w_up, ffn_w_down):
    batch, seq, d = x.shape
    depth = mix_pre_g.shape[0]
    hidden = ffn_w_gate.shape[2]
    m = batch * seq
    tn = 512
    tn_in = min(1024, d)
    main_cols = 6 * d + 2 * SSD_GROUPS * SSD_STATE
    n_ssd_heads = d // SSD_HEAD_DIM
    f_blocks = d // tn_in

    lb_p = jax.nn.softmax(hgrn_lb_logits.astype(F32), axis=0)
    lower_bounds = jnp.cumsum(lb_p, axis=0) - lb_p[0]

    w_in_t = jnp.swapaxes(even_w_in, 1, 2)
    w_out16 = even_w_out.astype(BF16)
    w_down16 = ffn_w_down.astype(BF16)
    w2_16 = conf_w2.astype(BF16)
    conf_b1_3d = conf_b1.reshape(conf_b1.shape[0], 1, -1)
    plus = lambda off: (lambda n: n + off)

    h = x.reshape(m, d)
    u = prenorm(h, mix_pre_g[0])
    for layer in range(depth):
        i = layer // 2
        if layer % 2 == 0:
            skip_f = lambda n: n + jnp.where(n >= f_blocks, f_blocks, 0)
            pb = wide_proj(u, [w_in_t], i, [skip_f], [], [], (main_cols - d) // tn_in,
                           _identity_epilogue, BF16, "even_in_proj", tn=tn_in,
                           w_is_transposed=True)
            f_pre = wide_proj(u, [w_in_t], i, [plus(f_blocks)], [], [], f_blocks,
                              _identity_epilogue, F32, "even_f_proj", tn=tn_in,
                              w_is_transposed=True)
            dt_raw = small_proj(u, w_in_t, i, main_cols, n_ssd_heads)
            o_a = hgrn_mix(pb, f_pre, lower_bounds[i], hgrn_norm_g[i], batch, seq, d)
            o_b = ssd_mix(pb, dt_raw, ssd_conv_w[i], ssd_conv_b[i], ssd_dt_bias[i],
                          ssd_a_log[i], ssd_d[i], ssd_norm_g[i], batch, seq, d)
            h, u = out_proj([o_a, o_b], w_out16, i, h, mix_post_g[layer],
                            ffn_pre_g[layer], True, "even_out_proj")
        else:
            c = wide_proj(u, [conf_w1, conf_w1], i, [plus(0), plus(d // tn)],
                          [conf_b1_3d, conf_b1_3d], [plus(0), plus(d // tn)], d // tn,
                          _glu_epilogue, F32, "conf_glu", tn=tn)
            h, u = conf_tail(c, conf_dw_w[i], conf_dw_b[i], conf_ln_g[i], conf_ln_b[i],
                             w2_16, i, conf_b2[i], h, mix_post_g[layer],
                             ffn_pre_g[layer], batch, seq)
        act = wide_proj(u, [ffn_w_gate, ffn_w_up], layer, [plus(0), plus(0)], [], [],
                        hidden // tn, _swiglu_epilogue, BF16, "ffn_in", tn=tn)
        last = layer == depth - 1
        gnext = mix_pre_g[layer] if last else mix_pre_g[layer + 1]
        h, u = out_proj([act], w_down16, layer, h, ffn_post_g[layer], gnext, not last,
                        "ffn_out")
    return h.reshape(batch, seq, d)
```

```python
import functools

import jax
import jax.numpy as jnp
from jax import lax
from jax.experimental import pallas as pl
from jax.experimental.pallas import tpu as pltpu

F32 = jnp.float32
BF16 = jnp.bfloat16

RMS_EPS = 1e-6
LN_EPS = 1e-5
HGRN_F_MIN = 1e-6
HGRN_HEAD = 128
HGRN_CHUNK = 64
HGRN_SUB = 8
HGRN_HEADS_PER_PART = 4
HGRN_PARTS_PER_STEP = 4
SSD_HEAD_DIM = 64
SSD_GROUPS = 4
SSD_STATE = 128
SSD_CONV = 4
SSD_CHUNK = 128
SSD_GROUPS_PER_STEP = 4
CONF_KERNEL = 31
CONF_HALO = 32
WEIGHT_STAGE_ROWS = 512
LANES = 128
VMEM_LIMIT = 56 * 1024 * 1024


def _cparams(semantics):
    return pltpu.CompilerParams(dimension_semantics=semantics,
                                vmem_limit_bytes=VMEM_LIMIT)


def _sigmoid(x):
    return 0.5 * jnp.tanh(0.5 * x) + 0.5


def _silu(x):
    t = 0.5 * x
    return t * jnp.tanh(t) + t


def _softplus(x):
    return jnp.maximum(x, 0.0) + jnp.log1p(jnp.exp(-jnp.abs(x)))


def _rms_rows(x, g, eps=RMS_EPS):
    ms = jnp.mean(x * x, axis=-1, keepdims=True)
    return x * lax.rsqrt(ms + eps) * g


def _dot(a, b):
    return jnp.dot(a, b, preferred_element_type=F32)


def _dot_nt(a, b):
    return lax.dot_general(a, b, (((1,), (1,)), ((), ())), preferred_element_type=F32)


def _dot_tn(a, b):
    return lax.dot_general(a, b, (((0,), (0,)), ((), ())), preferred_element_type=F32)


def _split3(x):
    hi = x.astype(BF16)
    r1 = x - hi.astype(F32)
    mid = r1.astype(BF16)
    lo = (r1 - mid.astype(F32)).astype(BF16)
    return hi, mid, lo


def _dot_exact_rhs(sel, x):
    return _dot(jnp.concatenate([sel, sel, sel], axis=1),
                jnp.concatenate(_split3(x), axis=0))


def _lane_stack3(x, n):
    hi, mid, lo = _split3(x)
    return (hi.astype(F32) + pltpu.roll(mid.astype(F32), n, 1)
            + pltpu.roll(lo.astype(F32), 2 * n, 1)).astype(BF16)


def _lower_tri(n, dtype):
    r = lax.broadcasted_iota(jnp.int32, (n, n), 0)
    c = lax.broadcasted_iota(jnp.int32, (n, n), 1)
    return (r >= c).astype(dtype)


def _prenorm_kernel(x_ref, g_ref, o_ref):
    o_ref[...] = _rms_rows(x_ref[...], g_ref[...]).astype(o_ref.dtype)


def prenorm(x, g, tm=512):
    m, d = x.shape
    tm = min(tm, m)
    return pl.pallas_call(
        _prenorm_kernel,
        grid=(m // tm,),
        in_specs=[pl.BlockSpec((tm, d), lambda i: (i, 0)),
                  pl.BlockSpec((1, d), lambda i: (0, 0))],
        out_specs=pl.BlockSpec((tm, d), lambda i: (i, 0)),
        out_shape=jax.ShapeDtypeStruct((m, d), BF16),
        compiler_params=_cparams(("arbitrary",)),
        name="prenorm",
    )(x, g.reshape(1, d))


def _wide_kernel(*refs, n_w, n_b, epilogue, w_is_transposed):
    u_ref = refs[0]
    w_refs = refs[1:1 + n_w]
    b_refs = refs[1 + n_w:1 + n_w + n_b]
    o_ref = refs[1 + n_w + n_b]
    s_refs = refs[2 + n_w + n_b:]

    @pl.when(pl.program_id(1) == 0)
    def _():
        for w, s in zip(w_refs, s_refs):
            s[...] = w[...].astype(BF16)

    u = u_ref[...]
    dot = _dot_nt if w_is_transposed else _dot
    ys = [dot(u, s[...]) for s in s_refs]
    o_ref[...] = epilogue(*ys, *[b[...] for b in b_refs]).astype(o_ref.dtype)


def wide_proj(u, ws, layer, w_col_maps, bs, b_col_maps, n_blocks, epilogue, out_dtype,
              name, tn=512, tm=1024, w_is_transposed=False):
    m, k = u.shape
    tm = min(tm, m)
    in_specs = [pl.BlockSpec((tm, k), lambda n, i: (i, 0))]
    for cmap in w_col_maps:
        if w_is_transposed:
            in_specs.append(pl.BlockSpec((None, tn, k), lambda n, i, cmap=cmap: (layer, cmap(n), 0)))
        else:
            in_specs.append(pl.BlockSpec((None, k, tn), lambda n, i, cmap=cmap: (layer, 0, cmap(n))))
    for cmap in b_col_maps:
        in_specs.append(pl.BlockSpec((None, 1, tn), lambda n, i, cmap=cmap: (layer, 0, cmap(n))))
    return pl.pallas_call(
        functools.partial(_wide_kernel, n_w=len(ws), n_b=len(bs), epilogue=epilogue,
                          w_is_transposed=w_is_transposed),
        grid=(n_blocks, m // tm),
        in_specs=in_specs,
        out_specs=pl.BlockSpec((tm, tn), lambda n, i: (i, n)),
        out_shape=jax.ShapeDtypeStruct((m, n_blocks * tn), out_dtype),
        scratch_shapes=[pltpu.VMEM((tn, k) if w_is_transposed else (k, tn), BF16) for _ in ws],
        compiler_params=_cparams(("arbitrary", "arbitrary")),
        name=name,
    )(u, *ws, *bs)


def _identity_epilogue(y):
    return y


def _swiglu_epilogue(g, up):
    return _silu(g) * up


def _glu_epilogue(a, g, ba, bg):
    return (a + ba) * _sigmoid(g + bg)


def _small_proj_kernel(u_ref, w_ref, o_ref):
    n, k = w_ref.shape
    w = jnp.concatenate([w_ref[...], jnp.zeros((LANES - n, k), F32)], axis=0)
    o_ref[...] = _dot_nt(u_ref[...], w.astype(BF16))


def small_proj(u, w_t_stack, layer, first_row, n, tm=1024):
    m, k = u.shape
    tm = min(tm, m)
    return pl.pallas_call(
        _small_proj_kernel,
        grid=(m // tm,),
        in_specs=[pl.BlockSpec((tm, k), lambda i: (i, 0)),
                  pl.BlockSpec((None, n, k), lambda i: (layer, first_row // n, 0))],
        out_specs=pl.BlockSpec((tm, LANES), lambda i: (i, 0)),
        out_shape=jax.ShapeDtypeStruct((m, LANES), F32),
        compiler_params=_cparams(("arbitrary",)),
        name="dt_proj",
    )(u, w_t_stack)


def _residual_epilogue(m, h, gpost, gnext, hout_ref, unext_ref):
    hn = h + _rms_rows(m, gpost)
    hout_ref[...] = hn
    if unext_ref is not None:
        unext_ref[...] = _rms_rows(hn, gnext).astype(unext_ref.dtype)


def _load_weight_bf16(w_hbm, layer, w16_ref, stage_ref, sem_ref):
    rows = stage_ref.shape[1]
    n_chunks = w16_ref.shape[0] // rows

    def chunk_copy(c):
        return pltpu.make_async_copy(w_hbm.at[layer, pl.ds(c * rows, rows), :],
                                     stage_ref.at[c % 2], sem_ref.at[c % 2])

    chunk_copy(0).start()
    for c in range(n_chunks):
        if c + 1 < n_chunks:
            chunk_copy(c + 1).start()
        chunk_copy(c).wait()
        w16_ref[c * rows:(c + 1) * rows, :] = stage_ref[c % 2].astype(BF16)


def _out_proj_kernel(*refs, n_a, with_next, layer):
    a_refs = refs[:n_a]
    w_hbm, h_ref, gpost_ref, gnext_ref, hout_ref = refs[n_a:n_a + 5]
    unext_ref = refs[n_a + 5] if with_next else None
    w16_ref, stage_ref, sem_ref = refs[-3:]

    @pl.when(pl.program_id(0) == 0)
    def _():
        _load_weight_bf16(w_hbm, layer, w16_ref, stage_ref, sem_ref)

    off = 0
    m = None
    for a_ref in a_refs:
        ka = a_ref.shape[1]
        part = _dot(a_ref[...], w16_ref[off:off + ka, :])
        m = part if m is None else m + part
        off += ka
    _residual_epilogue(m, h_ref[...], gpost_ref[...], gnext_ref[...], hout_ref, unext_ref)


def out_proj(acts, w_stack, layer, h, gpost, gnext, with_next, name, tm=256,
             stage_rows=WEIGHT_STAGE_ROWS):
    m = h.shape[0]
    _, kk, d = w_stack.shape
    tm = min(tm, m)
    stage_rows = min(stage_rows, kk)
    row = pl.BlockSpec((tm, d), lambda i: (i, 0))
    vec = pl.BlockSpec((1, d), lambda i: (0, 0))
    out_shape = [jax.ShapeDtypeStruct((m, d), F32)]
    out_specs = [row]
    if with_next:
        out_shape.append(jax.ShapeDtypeStruct((m, d), BF16))
        out_specs.append(row)
    res = pl.pallas_call(
        functools.partial(_out_proj_kernel, n_a=len(acts), with_next=with_next, layer=layer),
        grid=(m // tm,),
        in_specs=[pl.BlockSpec((tm, a.shape[1]), lambda i: (i, 0)) for a in acts] + [
            pl.BlockSpec(memory_space=pl.ANY), row, vec, vec],
        out_specs=out_specs,
        out_shape=out_shape,
        scratch_shapes=[pltpu.VMEM((kk, d), BF16), pltpu.VMEM((2, stage_rows, d), F32),
                        pltpu.SemaphoreType.DMA((2,))],
        compiler_params=_cparams(("arbitrary",)),
        name=name,
    )(*acts, w_stack, h, gpost.reshape(1, d), gnext.reshape(1, d))
    return (res[0], res[1]) if with_next else (res[0], None)


def _hgrn_kernel(q_ref, f_ref, v_ref, gate_ref, lb_ref, an_ref, o_ref, st_ref,
                 *, n_chunks, n_heads, part_heads):
    C, SUB, HD = HGRN_CHUNK, HGRN_SUB, HGRN_HEAD
    n_sub = C // SUB
    mid = SUB // 2 - 1
    width = part_heads * HD
    head_lanes = [slice(h * HD, (h + 1) * HD) for h in range(part_heads)]
    parts = [slice(p * width, (p + 1) * width) for p in range(n_heads // part_heads)]

    @pl.when(pl.program_id(2) == 0)
    def _():
        st_ref[...] = jnp.zeros_like(st_ref)

    tri = _lower_tri(C, BF16)
    causal = (lax.broadcasted_iota(jnp.int32, (C, C), 0)
              >= lax.broadcasted_iota(jnp.int32, (C, C), 1))
    def gates(sl, part):
        lb = lb_ref[:, part]
        sig = _sigmoid(f_ref[sl, part])
        f = lb + (1.0 - lb) * sig
        k = 1.0 - f
        log_f = jnp.log(jnp.maximum(f, HGRN_F_MIN))
        qs = _silu(q_ref[sl, part].astype(F32))
        b = _dot_exact_rhs(tri, log_f)
        return qs, k, b, v_ref[sl, part]

    def scores_and_state(vals, h0):
        qs, k, b, v16 = vals
        b_last = b[C - 1:C, :]

        q_in16 = (qs * jnp.exp(b)).astype(BF16)
        k_end16 = (k * jnp.exp(b_last - b)).astype(BF16)
        dec_last = jnp.exp(b_last)
        sts = [st_ref[h0 + h] for h in range(part_heads)]
        o_heads = [_dot_nt(q_in16[:, s], st.astype(BF16)) for s, st in zip(head_lanes, sts)]
        for h, (s, st) in enumerate(zip(head_lanes, sts)):
            st_ref[h0 + h] = st * dec_last[:, s] + _dot_tn(v16[:, s], k_end16[:, s])

        blocks = [[] for _ in range(part_heads)]
        keys = []
        prev_ref = None
        for i in range(n_sub):
            rows = slice(i * SUB, (i + 1) * SUB)
            ref_b = b[i * SUB + mid:i * SUB + mid + 1, :]
            if keys:
                step = jnp.exp(ref_b - prev_ref)
                keys = [kj * step for kj in keys]
            keys.append(k[rows, :] * jnp.exp(ref_b - b[rows, :]))
            prev_ref = ref_b
            qt16 = (qs[rows, :] * jnp.exp(b[rows, :] - ref_b)).astype(BF16)
            pad = [jnp.zeros(((n_sub - 1 - i) * SUB, width), F32)] if i < n_sub - 1 else []
            kt16 = jnp.concatenate(keys + pad, axis=0).astype(BF16)
            for h, s in enumerate(head_lanes):
                blocks[h].append(_dot_nt(qt16[:, s], kt16[:, s]))
        return o_heads, blocks

    def outputs(sl, part, vals, o_heads, blocks):
        v16 = vals[3]
        for h, s in enumerate(head_lanes):
            scores = jnp.where(causal, jnp.concatenate(blocks[h], axis=0), 0.0)
            o_heads[h] = o_heads[h] + _dot(scores.astype(BF16), v16[:, s])
        y = jnp.concatenate(
            [o * lax.rsqrt(jnp.mean(o * o, axis=-1, keepdims=True) + RMS_EPS) for o in o_heads],
            axis=1)
        y = y * an_ref[:, part] * _silu(gate_ref[sl, part].astype(F32))
        o_ref[sl, part] = y.astype(o_ref.dtype)

    def chunk(c, carry):
        sl = pl.ds(pl.multiple_of(c * C, C), C)
        vals = [gates(sl, part) for part in parts]
        mids = [scores_and_state(v, p * part_heads) for p, v in enumerate(vals)]
        for part, v, (o_heads, blocks) in zip(parts, vals, mids):
            outputs(sl, part, v, o_heads, blocks)
        return carry

    lax.fori_loop(0, n_chunks, chunk, 0)


def hgrn_mix(pb, f_pre, lb, a_norm, batch, seq, d_model, rows_per_step=512):
    n_heads = d_model // HGRN_HEAD
    part_heads = min(HGRN_HEADS_PER_PART, n_heads)
    hb = min(HGRN_PARTS_PER_STEP * part_heads, n_heads)
    gw = hb * HGRN_HEAD
    n_groups = n_heads // hb
    cs = min(rows_per_step, seq)
    steps = seq // cs
    m = batch * seq

    def col(seg):
        return pl.BlockSpec((cs, gw), lambda b, h, s, seg=seg: (b * steps + s, seg * n_groups + h))

    vec = pl.BlockSpec((1, gw), lambda b, h, s: (0, h))
    return pl.pallas_call(
        functools.partial(_hgrn_kernel, n_chunks=cs // HGRN_CHUNK, n_heads=hb,
                          part_heads=part_heads),
        grid=(batch, n_groups, steps),
        in_specs=[col(0), col(0), col(1), col(2), vec, vec],
        out_specs=col(0),
        out_shape=jax.ShapeDtypeStruct((m, d_model), BF16),
        scratch_shapes=[pltpu.VMEM((hb, HGRN_HEAD, HGRN_HEAD), F32)],
        compiler_params=_cparams(("arbitrary", "arbitrary", "arbitrary")),
        name="hgrn2",
    )(pb, f_pre, pb, pb, lb.reshape(1, d_model), a_norm.reshape(1, d_model))


def _causal_conv(raw_ref, carry_ref, ext_ref, w_ref, b_ref, n_rows, n_taps, halo):
    first = halo - n_taps + 1
    outs = []
    for j in range(raw_ref.shape[1] // LANES):
        ln = slice(j * LANES, (j + 1) * LANES)
        raw = raw_ref[:, ln].astype(F32)
        ext_ref[j, 0:halo, :] = carry_ref[j]
        ext_ref[j, halo:halo + n_rows, :] = raw
        carry_ref[j] = raw[n_rows - halo:n_rows, :]
        acc = b_ref[:, ln] + w_ref[0:1, ln] * ext_ref[j, pl.ds(first, n_rows), :]
        for k in range(1, n_taps):
            acc = acc + w_ref[k:k + 1, ln] * ext_ref[j, pl.ds(first + k, n_rows), :]
        outs.append(acc)
    return jnp.concatenate(outs, axis=1)


def _ssd_kernel(x_ref, bm_ref, cm_ref, z_ref, dt_ref,
                cwx_ref, cwb_ref, cwc_ref, cbx_ref, cbb_ref, cbc_ref,
                dtb_ref, alog_ref, selp_ref, selw_ref, dskip_ref, norm_ref, o_ref,
                st_ref, carx_ref, carb_ref, carc_ref, extx_ref, extb_ref, extc_ref,
                *, heads_per_group, n_groups, n_heads):
    L, P, R, N = SSD_CHUNK, SSD_HEAD_DIM, heads_per_group, SSD_STATE
    gw = R * P
    groups = range(n_groups)
    g_lanes = [slice(j * gw, (j + 1) * gw) for j in groups]
    n_lanes = [slice(j * N, (j + 1) * N) for j in groups]

    @pl.when(pl.program_id(2) == 0)
    def _():
        st_ref[...] = jnp.zeros_like(st_ref)
        carx_ref[...] = jnp.zeros_like(carx_ref)
        carb_ref[...] = jnp.zeros_like(carb_ref)
        carc_ref[...] = jnp.zeros_like(carc_ref)

    halo = carx_ref.shape[1]
    xc = _silu(_causal_conv(x_ref, carx_ref, extx_ref, cwx_ref, cbx_ref, L, SSD_CONV, halo))
    bc16 = _silu(_causal_conv(bm_ref, carb_ref, extb_ref, cwb_ref, cbb_ref, L, SSD_CONV,
                              halo)).astype(BF16)
    cc16 = _silu(_causal_conv(cm_ref, carc_ref, extc_ref, cwc_ref, cbc_ref, L, SSD_CONV,
                              halo)).astype(BF16)

    lane = lax.broadcasted_iota(jnp.int32, (L, LANES), 1)
    dt = jnp.where(lane < n_heads, _softplus(dt_ref[...] + dtb_ref[...]), 0.0)
    dta = dt * (-jnp.exp(alog_ref[...]))
    cs = _dot_exact_rhs(_lower_tri(L, BF16), dta)

    dt3 = _lane_stack3(dt, n_heads)
    cs3 = _lane_stack3(cs, n_heads)
    dt_x = [_dot(dt3, selp_ref[j]) for j in groups]
    cs_x = [_dot(cs3, selp_ref[j]) for j in groups]
    cs_w = [_dot(cs3, selw_ref[j]) for j in groups]

    rr = lax.broadcasted_iota(jnp.int32, (L, L), 0)
    cl = lax.broadcasted_iota(jnp.int32, (L, L), 1)
    causal = rr >= cl
    ys, xdt16s, cbs = [], [], []
    for j in groups:
        xdt = xc[:, g_lanes[j]] * dt_x[j]
        cs_last = cs_x[j][L - 1:L, :]
        st = st_ref[j]
        ys.append(_dot(cc16[:, n_lanes[j]], st.astype(BF16)) * jnp.exp(cs_x[j]))
        st_ref[j] = st * jnp.exp(cs_last) + _dot_tn(
            bc16[:, n_lanes[j]], (xdt * jnp.exp(cs_last - cs_x[j])).astype(BF16))
        cbs.append(_dot_nt(cc16[:, n_lanes[j]], bc16[:, n_lanes[j]]))
        xdt16s.append(xdt.astype(BF16))

    parts = [[] for _ in groups]
    for r in range(R):
        for j in groups:
            col = cs_w[j][:, r * L:(r + 1) * L]
            seg = col - col.T
            decay = jnp.where(causal, jnp.exp(jnp.where(causal, seg, 0.0)), 0.0)
            parts[j].append(_dot((cbs[j] * decay).astype(BF16),
                                 xdt16s[j][:, r * P:(r + 1) * P]))

    for j in groups:
        ln = g_lanes[j]
        y = ys[j] + jnp.concatenate(parts[j], axis=1) + dskip_ref[:, ln] * xc[:, ln]
        yz = y * _silu(z_ref[:, ln].astype(F32))
        o_ref[:, ln] = _rms_rows(yz, norm_ref[:, ln]).astype(o_ref.dtype)


def ssd_mix(pb, dt_raw, conv_w, conv_b, dt_bias, a_log, d_skip, b_norm,
            batch, seq, d_model):
    L, G, N = SSD_CHUNK, SSD_GROUPS, SSD_STATE
    n_heads = d_model // SSD_HEAD_DIM
    R = n_heads // G
    gw = d_model // G
    steps = seq // L
    m = batch * seq
    halo = 8
    gs = SSD_GROUPS_PER_STEP
    gsw, gsn = gs * gw, gs * N
    bn0 = 5 * d_model // gsn
    cwb0 = d_model // gsn

    def rep_heads(v):
        return jnp.repeat(v.astype(F32), SSD_HEAD_DIM).reshape(1, d_model)

    def pad_heads(v):
        return jnp.pad(v.astype(F32), (0, LANES - n_heads)).reshape(1, LANES)

    def select(width):
        row = jnp.arange(LANES, dtype=jnp.int32)[None, :, None]
        lane = jnp.arange(R * width, dtype=jnp.int32)[None, None, :]
        grp = jnp.arange(G, dtype=jnp.int32)[:, None, None]
        hit = (row % n_heads == grp * R + lane // width) & (row < 3 * n_heads)
        return hit.astype(BF16)

    row_g = lambda off: pl.BlockSpec((L, gsw), lambda b, g, s, off=off: (b * steps + s, off + g))
    row_n = lambda off: pl.BlockSpec((L, gsn), lambda b, g, s, off=off: (b * steps + s, off + g))
    par_g = lambda rows: pl.BlockSpec((rows, gsw), lambda b, g, s: (0, g))
    par_n = lambda rows, off: pl.BlockSpec((rows, gsn), lambda b, g, s, off=off: (0, off + g))
    head_vec = pl.BlockSpec((1, LANES), lambda b, g, s: (0, 0))
    sel = lambda width: pl.BlockSpec((gs, LANES, R * width), lambda b, g, s: (g, 0, 0))
    cb2 = conv_b.reshape(1, -1)
    n_steps_g = G // gs

    return pl.pallas_call(
        functools.partial(_ssd_kernel, heads_per_group=R, n_groups=gs, n_heads=n_heads),
        grid=(batch, n_steps_g, steps),
        in_specs=[row_g(4 * n_steps_g), row_n(bn0), row_n(bn0 + n_steps_g), row_g(3 * n_steps_g),
                  pl.BlockSpec((L, LANES), lambda b, g, s: (b * steps + s, 0)),
                  par_g(SSD_CONV), par_n(SSD_CONV, cwb0), par_n(SSD_CONV, cwb0 + n_steps_g),
                  par_g(1), par_n(1, cwb0), par_n(1, cwb0 + n_steps_g),
                  head_vec, head_vec, sel(SSD_HEAD_DIM), sel(L), par_g(1), par_g(1)],
        out_specs=pl.BlockSpec((L, gsw), lambda b, g, s: (b * steps + s, g)),
        out_shape=jax.ShapeDtypeStruct((m, d_model), BF16),
        scratch_shapes=[pltpu.VMEM((gs, N, gw), F32),
                        pltpu.VMEM((gsw // LANES, halo, LANES), F32),
                        pltpu.VMEM((gsn // LANES, halo, LANES), F32),
                        pltpu.VMEM((gsn // LANES, halo, LANES), F32),
                        pltpu.VMEM((gsw // LANES, halo + L, LANES), F32),
                        pltpu.VMEM((gsn // LANES, halo + L, LANES), F32),
                        pltpu.VMEM((gsn // LANES, halo + L, LANES), F32)],
        compiler_params=_cparams(("arbitrary", "arbitrary", "arbitrary")),
        name="ssd",
    )(pb, pb, pb, pb, dt_raw,
      conv_w, conv_w, conv_w, cb2, cb2, cb2,
      pad_heads(dt_bias), pad_heads(a_log),
      select(SSD_HEAD_DIM), select(L), rep_heads(d_skip), b_norm.reshape(1, d_model))


def _conf_tail_kernel(c_ref, dww_ref, dwb_ref, lng_ref, lnb_ref, w2_hbm, b2_ref,
                      h_ref, gpost_ref, gnext_ref, hout_ref, unext_ref,
                      carry_ref, ext_ref, conv_ref, w2_ref, stage_ref, sem_ref,
                      *, row_block, layer):
    tm, d = c_ref.shape
    halo = CONF_HALO
    first = halo - (CONF_KERNEL - 1)
    n_tiles = d // LANES

    @pl.when((pl.program_id(0) == 0) & (pl.program_id(1) == 0))
    def _():
        _load_weight_bf16(w2_hbm, layer, w2_ref, stage_ref, sem_ref)

    @pl.when(pl.program_id(1) == 0)
    def _():
        carry_ref[...] = jnp.zeros_like(carry_ref)

    for j in range(n_tiles):
        ln = slice(j * LANES, (j + 1) * LANES)
        ext_ref[j, 0:halo, :] = carry_ref[j]
        ext_ref[j, halo:halo + tm, :] = c_ref[:, ln]
        carry_ref[j] = c_ref[tm - halo:tm, ln]

    def col_tile(j, carry):
        cj = pl.ds(pl.multiple_of(j * LANES, LANES), LANES)
        for rb in range(tm // row_block):
            base = rb * row_block
            acc = dwb_ref[:, cj] + dww_ref[0:1, cj] * ext_ref[j, pl.ds(base + first, row_block), :]
            for k in range(1, CONF_KERNEL):
                acc = acc + dww_ref[k:k + 1, cj] * ext_ref[j, pl.ds(base + first + k, row_block), :]
            conv_ref[base:base + row_block, cj] = acc
        return carry

    lax.fori_loop(0, n_tiles, col_tile, 0)

    x = conv_ref[...]
    mu = jnp.mean(x, axis=-1, keepdims=True)
    xc = x - mu
    var = jnp.mean(xc * xc, axis=-1, keepdims=True)
    y = _silu(xc * lax.rsqrt(var + LN_EPS) * lng_ref[...] + lnb_ref[...])
    m = _dot(y.astype(BF16), w2_ref[...]) + b2_ref[...]
    _residual_epilogue(m, h_ref[...], gpost_ref[...], gnext_ref[...], hout_ref, unext_ref)


def conf_tail(c, dw_w, dw_b, ln_g, ln_b, w2_stack, layer, b2, h, gpost, gnext,
              batch, seq, tm=256, row_block=64, stage_rows=WEIGHT_STAGE_ROWS):
    m, d = c.shape
    tm = min(tm, seq)
    row_block = min(row_block, tm)
    stage_rows = min(stage_rows, d)
    steps = seq // tm
    row = pl.BlockSpec((tm, d), lambda b, s: (b * steps + s, 0))
    vec = pl.BlockSpec((1, d), lambda b, s: (0, 0))
    dww = jnp.pad(dw_w, ((0, CONF_HALO - CONF_KERNEL), (0, 0)))
    return pl.pallas_call(
        functools.partial(_conf_tail_kernel, row_block=row_block, layer=layer),
        grid=(batch, steps),
        in_specs=[row, pl.BlockSpec((CONF_HALO, d), lambda b, s: (0, 0)), vec, vec, vec,
                  pl.BlockSpec(memory_space=pl.ANY),
                  vec, row, vec, vec],
        out_specs=[row, row],
        out_shape=[jax.ShapeDtypeStruct((m, d), F32), jax.ShapeDtypeStruct((m, d), BF16)],
        scratch_shapes=[pltpu.VMEM((d // LANES, CONF_HALO, LANES), F32),
                        pltpu.VMEM((d // LANES, CONF_HALO + tm, LANES), F32),
                        pltpu.VMEM((tm, d), F32),
                        pltpu.VMEM((d, d), BF16), pltpu.VMEM((2, stage_rows, d), F32),
                        pltpu.SemaphoreType.DMA((2,))],
        compiler_params=_cparams(("arbitrary", "arbitrary")),
        name="conf_tail",
    )(c, dww, dw_b.reshape(1, d), ln_g.reshape(1, d), ln_b.reshape(1, d), w2_stack,
      b2.reshape(1, d), h, gpost.reshape(1, d), gnext.reshape(1, d))


def kernel(x, mix_pre_g, mix_post_g, ffn_pre_g, ffn_post_g, hgrn_lb_logits, even_w_in,
           hgrn_norm_g, ssd_conv_w, ssd_conv_b, ssd_dt_bias, ssd_a_log, ssd_d, ssd_norm_g,
           even_w_out, conf_w1, conf_b1, conf_dw_w, conf_dw_b, conf_ln_g, conf_ln_b,
           conf_w2, conf_b2, ffn_w_gate, ffn_w_up, ffn_w_down):
    batch, seq, d = x.shape
    depth = mix_pre_g.shape[0]
    hidden = ffn_w_gate.shape[2]
    m = batch * seq
    tn = 512
    tn_in = min(1024, d)
    main_cols = 6 * d + 2 * SSD_GROUPS * SSD_STATE
    n_ssd_heads = d // SSD_HEAD_DIM
    f_blocks = d // tn_in

    lb_p = jax.nn.softmax(hgrn_lb_logits.astype(F32), axis=0)
    lower_bounds = jnp.cumsum(lb_p, axis=0) - lb_p[0]

    w_in_t = jnp.swapaxes(even_w_in, 1, 2)
    conf_b1_3d = conf_b1.reshape(conf_b1.shape[0], 1, -1)
    plus = lambda off: (lambda n: n + off)

    h = x.reshape(m, d)
    u = prenorm(h, mix_pre_g[0])
    for layer in range(depth):
        i = layer // 2
        if layer % 2 == 0:
            skip_f = lambda n: n + jnp.where(n >= f_blocks, f_blocks, 0)
            pb = wide_proj(u, [w_in_t], i, [skip_f], [], [], (main_cols - d) // tn_in,
                           _identity_epilogue, BF16, "even_in_proj", tn=tn_in,
                           w_is_transposed=True)
            f_pre = wide_proj(u, [w_in_t], i, [plus(f_blocks)], [], [], f_blocks,
                              _identity_epilogue, F32, "even_f_proj", tn=tn_in,
                              w_is_transposed=True)
            dt_raw = small_proj(u, w_in_t, i, main_cols, n_ssd_heads)
            o_a = hgrn_mix(pb, f_pre, lower_bounds[i], hgrn_norm_g[i], batch, seq, d)
            o_b = ssd_mix(pb, dt_raw, ssd_conv_w[i], ssd_conv_b[i], ssd_dt_bias[i],
                          ssd_a_log[i], ssd_d[i], ssd_norm_g[i], batch, seq, d)
            h, u = out_proj([o_a, o_b], even_w_out, i, h, mix_post_g[layer],
                            ffn_pre_g[layer], True, "even_out_proj")
        else:
            c = wide_proj(u, [conf_w1, conf_w1], i, [plus(0), plus(d // tn)],
                          [conf_b1_3d, conf_b1_3d], [plus(0), plus(d // tn)], d // tn,
                          _glu_epilogue, F32, "conf_glu", tn=tn)
            h, u = conf_tail(c, conf_dw_w[i], conf_dw_b[i], conf_ln_g[i], conf_ln_b[i],
                             conf_w2, i, conf_b2[i], h, mix_post_g[layer],
                             ffn_pre_g[layer], batch, seq)
        act = wide_proj(u, [ffn_w_gate, ffn_w_up], layer, [plus(0), plus(0)], [], [],
                        hidden // tn, _swiglu_epilogue, BF16, "ffn_in", tn=tn)
        last = layer == depth - 1
        gnext = mix_pre_g[layer] if last else mix_pre_g[layer + 1]
        h, u = out_proj([act], ffn_w_down, layer, h, ffn_post_g[layer], gnext, not last,
                        "ffn_out")
    return h.reshape(batch, seq, d)
```

```python
import functools

import jax
import jax.numpy as jnp
from jax import lax
from jax.experimental import pallas as pl
from jax.experimental.pallas import tpu as pltpu

F32 = jnp.float32
BF16 = jnp.bfloat16

RMS_EPS = 1e-6
LN_EPS = 1e-5
HGRN_F_MIN = 1e-6
HGRN_HEAD = 128
HGRN_CHUNK = 64
HGRN_SUB = 8
HGRN_HEADS_PER_PART = 4
HGRN_PARTS_PER_STEP = 4
SSD_HEAD_DIM = 64
SSD_GROUPS = 4
SSD_STATE = 128
SSD_CONV = 4
SSD_CHUNK = 128
SSD_GROUPS_PER_STEP = 4
CONF_KERNEL = 31
CONF_HALO = 32
WEIGHT_STAGE_ROWS = 512
LANES = 128
VMEM_LIMIT = 56 * 1024 * 1024


def _cparams(semantics):
    return pltpu.CompilerParams(dimension_semantics=semantics,
                                vmem_limit_bytes=VMEM_LIMIT)


def _sigmoid(x):
    return 0.5 * jnp.tanh(0.5 * x) + 0.5


def _silu(x):
    t = 0.5 * x
    return t * jnp.tanh(t) + t


def _softplus(x):
    return jnp.maximum(x, 0.0) + jnp.log1p(jnp.exp(-jnp.abs(x)))


def _rms_rows(x, g, eps=RMS_EPS):
    ms = jnp.mean(x * x, axis=-1, keepdims=True)
    return x * lax.rsqrt(ms + eps) * g


def _dot(a, b):
    return jnp.dot(a, b, preferred_element_type=F32)


def _dot_nt(a, b):
    return lax.dot_general(a, b, (((1,), (1,)), ((), ())), preferred_element_type=F32)


def _dot_tn(a, b):
    return lax.dot_general(a, b, (((0,), (0,)), ((), ())), preferred_element_type=F32)


def _split3(x):
    hi = x.astype(BF16)
    r1 = x - hi.astype(F32)
    mid = r1.astype(BF16)
    lo = (r1 - mid.astype(F32)).astype(BF16)
    return hi, mid, lo


def _dot_exact_rhs(sel, x):
    return _dot(jnp.concatenate([sel, sel, sel], axis=1),
                jnp.concatenate(_split3(x), axis=0))


def _lane_stack3(x, n):
    hi, mid, lo = _split3(x)
    return (hi.astype(F32) + pltpu.roll(mid.astype(F32), n, 1)
            + pltpu.roll(lo.astype(F32), 2 * n, 1)).astype(BF16)


def _lower_tri(n, dtype):
    r = lax.broadcasted_iota(jnp.int32, (n, n), 0)
    c = lax.broadcasted_iota(jnp.int32, (n, n), 1)
    return (r >= c).astype(dtype)


def _prenorm_kernel(x_ref, g_ref, o_ref):
    o_ref[...] = _rms_rows(x_ref[...], g_ref[...]).astype(o_ref.dtype)


def prenorm(x, g, tm=512):
    m, d = x.shape
    tm = min(tm, m)
    return pl.pallas_call(
        _prenorm_kernel,
        grid=(m // tm,),
        in_specs=[pl.BlockSpec((tm, d), lambda i: (i, 0)),
                  pl.BlockSpec((1, d), lambda i: (0, 0))],
        out_specs=pl.BlockSpec((tm, d), lambda i: (i, 0)),
        out_shape=jax.ShapeDtypeStruct((m, d), BF16),
        compiler_params=_cparams(("arbitrary",)),
        name="prenorm",
    )(x, g.reshape(1, d))


def _wide_kernel(*refs, n_w, n_b, epilogue, w_is_transposed):
    u_ref = refs[0]
    w_refs = refs[1:1 + n_w]
    b_refs = refs[1 + n_w:1 + n_w + n_b]
    o_ref = refs[1 + n_w + n_b]
    s_refs = refs[2 + n_w + n_b:]

    @pl.when(pl.program_id(1) == 0)
    def _():
        for w, s in zip(w_refs, s_refs):
            s[...] = w[...].astype(BF16)

    u = u_ref[...]
    dot = _dot_nt if w_is_transposed else _dot
    ys = [dot(u, s[...]) for s in s_refs]
    o_ref[...] = epilogue(*ys, *[b[...] for b in b_refs]).astype(o_ref.dtype)


def wide_proj(u, ws, layer, w_col_maps, bs, b_col_maps, n_blocks, epilogue, out_dtype,
              name, tn=512, tm=1024, w_is_transposed=False):
    m, k = u.shape
    tm = min(tm, m)
    in_specs = [pl.BlockSpec((tm, k), lambda n, i: (i, 0))]
    for cmap in w_col_maps:
        if w_is_transposed:
            in_specs.append(pl.BlockSpec((None, tn, k), lambda n, i, cmap=cmap: (layer, cmap(n), 0)))
        else:
            in_specs.append(pl.BlockSpec((None, k, tn), lambda n, i, cmap=cmap: (layer, 0, cmap(n))))
    for cmap in b_col_maps:
        in_specs.append(pl.BlockSpec((None, 1, tn), lambda n, i, cmap=cmap: (layer, 0, cmap(n))))
    return pl.pallas_call(
        functools.partial(_wide_kernel, n_w=len(ws), n_b=len(bs), epilogue=epilogue,
                          w_is_transposed=w_is_transposed),
        grid=(n_blocks, m // tm),
        in_specs=in_specs,
        out_specs=pl.BlockSpec((tm, tn), lambda n, i: (i, n)),
        out_shape=jax.ShapeDtypeStruct((m, n_blocks * tn), out_dtype),
        scratch_shapes=[pltpu.VMEM((tn, k) if w_is_transposed else (k, tn), BF16) for _ in ws],
        compiler_params=_cparams(("arbitrary", "arbitrary")),
        name=name,
    )(u, *ws, *bs)


def _identity_epilogue(y):
    return y


def _swiglu_epilogue(g, up):
    return _silu(g) * up


def _glu_epilogue(a, g, ba, bg):
    return (a + ba) * _sigmoid(g + bg)


def _small_proj_kernel(u_ref, w_ref, o_ref):
    n, k = w_ref.shape
    w = jnp.concatenate([w_ref[...], jnp.zeros((LANES - n, k), F32)], axis=0)
    o_ref[...] = _dot_nt(u_ref[...], w.astype(BF16))


def small_proj(u, w_t_stack, layer, first_row, n, tm=1024):
    m, k = u.shape
    tm = min(tm, m)
    return pl.pallas_call(
        _small_proj_kernel,
        grid=(m // tm,),
        in_specs=[pl.BlockSpec((tm, k), lambda i: (i, 0)),
                  pl.BlockSpec((None, n, k), lambda i: (layer, first_row // n, 0))],
        out_specs=pl.BlockSpec((tm, LANES), lambda i: (i, 0)),
        out_shape=jax.ShapeDtypeStruct((m, LANES), F32),
        compiler_params=_cparams(("arbitrary",)),
        name="dt_proj",
    )(u, w_t_stack)


def _residual_epilogue(m, h, gpost, gnext, hout_ref, unext_ref):
    hn = h + _rms_rows(m, gpost)
    hout_ref[...] = hn
    if unext_ref is not None:
        unext_ref[...] = _rms_rows(hn, gnext).astype(unext_ref.dtype)


def _load_weight_bf16(w_hbm, layer, w16_ref, stage_ref, sem_ref):
    rows = stage_ref.shape[1]
    n_chunks = w16_ref.shape[0] // rows

    def chunk_copy(c):
        return pltpu.make_async_copy(w_hbm.at[layer, pl.ds(c * rows, rows), :],
                                     stage_ref.at[c % 2], sem_ref.at[c % 2])

    chunk_copy(0).start()
    for c in range(n_chunks):
        if c + 1 < n_chunks:
            chunk_copy(c + 1).start()
        chunk_copy(c).wait()
        w16_ref[c * rows:(c + 1) * rows, :] = stage_ref[c % 2].astype(BF16)


def _out_proj_kernel(*refs, n_a, with_next, layer):
    a_refs = refs[:n_a]
    w_hbm, h_ref, gpost_ref, gnext_ref, hout_ref = refs[n_a:n_a + 5]
    unext_ref = refs[n_a + 5] if with_next else None
    w16_ref, stage_ref, sem_ref = refs[-3:]

    @pl.when(pl.program_id(0) == 0)
    def _():
        _load_weight_bf16(w_hbm, layer, w16_ref, stage_ref, sem_ref)

    off = 0
    m = None
    for a_ref in a_refs:
        ka = a_ref.shape[1]
        part = _dot(a_ref[...], w16_ref[off:off + ka, :])
        m = part if m is None else m + part
        off += ka
    _residual_epilogue(m, h_ref[...], gpost_ref[...], gnext_ref[...], hout_ref, unext_ref)


def out_proj(acts, w_stack, layer, h, gpost, gnext, with_next, name, tm=256,
             stage_rows=WEIGHT_STAGE_ROWS):
    m = h.shape[0]
    _, kk, d = w_stack.shape
    tm = min(tm, m)
    stage_rows = min(stage_rows, kk)
    row = pl.BlockSpec((tm, d), lambda i: (i, 0))
    vec = pl.BlockSpec((1, d), lambda i: (0, 0))
    out_shape = [jax.ShapeDtypeStruct((m, d), F32)]
    out_specs = [row]
    if with_next:
        out_shape.append(jax.ShapeDtypeStruct((m, d), BF16))
        out_specs.append(row)
    res = pl.pallas_call(
        functools.partial(_out_proj_kernel, n_a=len(acts), with_next=with_next, layer=layer),
        grid=(m // tm,),
        in_specs=[pl.BlockSpec((tm, a.shape[1]), lambda i: (i, 0)) for a in acts] + [
            pl.BlockSpec(memory_space=pl.ANY), row, vec, vec],
        out_specs=out_specs,
        out_shape=out_shape,
        scratch_shapes=[pltpu.VMEM((kk, d), BF16), pltpu.VMEM((2, stage_rows, d), F32),
                        pltpu.SemaphoreType.DMA((2,))],
        compiler_params=_cparams(("arbitrary",)),
        name=name,
    )(*acts, w_stack, h, gpost.reshape(1, d), gnext.reshape(1, d))
    return (res[0], res[1]) if with_next else (res[0], None)


def _hgrn_kernel(q_ref, f_ref, v_ref, gate_ref, lb_ref, an_ref, o_ref, st_ref,
                 *, n_chunks, n_heads, part_heads):
    C, SUB, HD = HGRN_CHUNK, HGRN_SUB, HGRN_HEAD
    n_sub = C // SUB
    mid = SUB // 2 - 1
    width = part_heads * HD
    head_lanes = [slice(h * HD, (h + 1) * HD) for h in range(part_heads)]
    parts = [slice(p * width, (p + 1) * width) for p in range(n_heads // part_heads)]

    @pl.when(pl.program_id(2) == 0)
    def _():
        st_ref[...] = jnp.zeros_like(st_ref)

    tri = _lower_tri(C, BF16)
    causal = (lax.broadcasted_iota(jnp.int32, (C, C), 0)
              >= lax.broadcasted_iota(jnp.int32, (C, C), 1))
    def gates(sl, part):
        lb = lb_ref[:, part]
        sig = _sigmoid(f_ref[sl, part])
        f = lb + (1.0 - lb) * sig
        k = 1.0 - f
        log_f = jnp.log(jnp.maximum(f, HGRN_F_MIN))
        qs = _silu(q_ref[sl, part].astype(F32))
        b = _dot_exact_rhs(tri, log_f)
        return qs, k, b, v_ref[sl, part]

    def scores_and_state(vals, h0):
        qs, k, b, v16 = vals
        b_last = b[C - 1:C, :]

        q_in16 = (qs * jnp.exp(b)).astype(BF16)
        k_end16 = (k * jnp.exp(b_last - b)).astype(BF16)
        dec_last = jnp.exp(b_last)
        sts = [st_ref[h0 + h] for h in range(part_heads)]
        o_heads = [_dot(q_in16[:, s], st.astype(BF16)) for s, st in zip(head_lanes, sts)]
        for h, (s, st) in enumerate(zip(head_lanes, sts)):
            row_decay = jnp.broadcast_to(dec_last[:, s], (HD, HD)).T
            st_ref[h0 + h] = st * row_decay + _dot_tn(k_end16[:, s], v16[:, s])

        blocks = [[] for _ in range(part_heads)]
        keys = []
        prev_ref = None
        for i in range(n_sub):
            rows = slice(i * SUB, (i + 1) * SUB)
            ref_b = b[i * SUB + mid:i * SUB + mid + 1, :]
            if keys:
                step = jnp.exp(ref_b - prev_ref)
                keys = [kj * step for kj in keys]
            keys.append(k[rows, :] * jnp.exp(ref_b - b[rows, :]))
            prev_ref = ref_b
            qt16 = (qs[rows, :] * jnp.exp(b[rows, :] - ref_b)).astype(BF16)
            pad = [jnp.zeros(((n_sub - 1 - i) * SUB, width), F32)] if i < n_sub - 1 else []
            kt16 = jnp.concatenate(keys + pad, axis=0).astype(BF16)
            for h, s in enumerate(head_lanes):
                blocks[h].append(_dot_nt(qt16[:, s], kt16[:, s]))
        return o_heads, blocks

    def outputs(sl, part, vals, o_heads, blocks):
        v16 = vals[3]
        for h, s in enumerate(head_lanes):
            scores = jnp.where(causal, jnp.concatenate(blocks[h], axis=0), 0.0)
            o_heads[h] = o_heads[h] + _dot(scores.astype(BF16), v16[:, s])
        y = jnp.concatenate(
            [o * lax.rsqrt(jnp.mean(o * o, axis=-1, keepdims=True) + RMS_EPS) for o in o_heads],
            axis=1)
        y = y * an_ref[:, part] * _silu(gate_ref[sl, part].astype(F32))
        o_ref[sl, part] = y.astype(o_ref.dtype)

    def chunk(c, carry):
        sl = pl.ds(pl.multiple_of(c * C, C), C)
        vals = [gates(sl, part) for part in parts]
        mids = {}
        for p in range(len(parts) + 1):
            if p < len(parts):
                mids[p] = scores_and_state(vals[p], p * part_heads)
            if p >= 1:
                outputs(sl, parts[p - 1], vals[p - 1], *mids[p - 1])
        return carry

    lax.fori_loop(0, n_chunks, chunk, 0)


def hgrn_mix(pb, f_pre, lb, a_norm, batch, seq, d_model, rows_per_step=512):
    n_heads = d_model // HGRN_HEAD
    part_heads = min(HGRN_HEADS_PER_PART, n_heads)
    hb = min(HGRN_PARTS_PER_STEP * part_heads, n_heads)
    gw = hb * HGRN_HEAD
    n_groups = n_heads // hb
    cs = min(rows_per_step, seq)
    steps = seq // cs
    m = batch * seq

    def col(seg):
        return pl.BlockSpec((cs, gw), lambda b, h, s, seg=seg: (b * steps + s, seg * n_groups + h))

    vec = pl.BlockSpec((1, gw), lambda b, h, s: (0, h))
    return pl.pallas_call(
        functools.partial(_hgrn_kernel, n_chunks=cs // HGRN_CHUNK, n_heads=hb,
                          part_heads=part_heads),
        grid=(batch, n_groups, steps),
        in_specs=[col(0), col(0), col(1), col(2), vec, vec],
        out_specs=col(0),
        out_shape=jax.ShapeDtypeStruct((m, d_model), BF16),
        scratch_shapes=[pltpu.VMEM((hb, HGRN_HEAD, HGRN_HEAD), F32)],
        compiler_params=_cparams(("arbitrary", "arbitrary", "arbitrary")),
        name="hgrn2",
    )(pb, f_pre, pb, pb, lb.reshape(1, d_model), a_norm.reshape(1, d_model))


def _causal_conv(raw_ref, carry_ref, ext_ref, w_ref, b_ref, n_rows, n_taps, halo):
    first = halo - n_taps + 1
    outs = []
    for j in range(raw_ref.shape[1] // LANES):
        ln = slice(j * LANES, (j + 1) * LANES)
        raw = raw_ref[:, ln].astype(F32)
        ext_ref[j, 0:halo, :] = carry_ref[j]
        ext_ref[j, halo:halo + n_rows, :] = raw
        carry_ref[j] = raw[n_rows - halo:n_rows, :]
        acc = b_ref[:, ln] + w_ref[0:1, ln] * ext_ref[j, pl.ds(first, n_rows), :]
        for k in range(1, n_taps):
            acc = acc + w_ref[k:k + 1, ln] * ext_ref[j, pl.ds(first + k, n_rows), :]
        outs.append(acc)
    return jnp.concatenate(outs, axis=1)


def _ssd_kernel(x_ref, bm_ref, cm_ref, z_ref, dt_ref,
                cwx_ref, cwb_ref, cwc_ref, cbx_ref, cbb_ref, cbc_ref,
                dtb_ref, alog_ref, selp_ref, selw_ref, dskip_ref, norm_ref, o_ref,
                st_ref, carx_ref, carb_ref, carc_ref, extx_ref, extb_ref, extc_ref,
                *, heads_per_group, n_groups, n_heads):
    L, P, R, N = SSD_CHUNK, SSD_HEAD_DIM, heads_per_group, SSD_STATE
    gw = R * P
    groups = range(n_groups)
    g_lanes = [slice(j * gw, (j + 1) * gw) for j in groups]
    n_lanes = [slice(j * N, (j + 1) * N) for j in groups]

    @pl.when(pl.program_id(2) == 0)
    def _():
        st_ref[...] = jnp.zeros_like(st_ref)
        carx_ref[...] = jnp.zeros_like(carx_ref)
        carb_ref[...] = jnp.zeros_like(carb_ref)
        carc_ref[...] = jnp.zeros_like(carc_ref)

    halo = carx_ref.shape[1]
    xc = _silu(_causal_conv(x_ref, carx_ref, extx_ref, cwx_ref, cbx_ref, L, SSD_CONV, halo))
    bc16 = _silu(_causal_conv(bm_ref, carb_ref, extb_ref, cwb_ref, cbb_ref, L, SSD_CONV,
                              halo)).astype(BF16)
    cc16 = _silu(_causal_conv(cm_ref, carc_ref, extc_ref, cwc_ref, cbc_ref, L, SSD_CONV,
                              halo)).astype(BF16)

    lane = lax.broadcasted_iota(jnp.int32, (L, LANES), 1)
    dt = jnp.where(lane < n_heads, _softplus(dt_ref[...] + dtb_ref[...]), 0.0)
    dta = dt * (-jnp.exp(alog_ref[...]))
    cs = _dot_exact_rhs(_lower_tri(L, BF16), dta)

    dt3 = _lane_stack3(dt, n_heads)
    cs3 = _lane_stack3(cs, n_heads)
    dt_x = [_dot(dt3, selp_ref[j]) for j in groups]
    cs_x = [_dot(cs3, selp_ref[j]) for j in groups]
    cs_w = [_dot(cs3, selw_ref[j]) for j in groups]

    rr = lax.broadcasted_iota(jnp.int32, (L, L), 0)
    cl = lax.broadcasted_iota(jnp.int32, (L, L), 1)
    causal = rr >= cl
    ys, xdt16s, cbs = [], [], []
    for j in groups:
        xdt = xc[:, g_lanes[j]] * dt_x[j]
        cs_last = cs_x[j][L - 1:L, :]
        st = st_ref[j]
        ys.append(_dot(cc16[:, n_lanes[j]], st.astype(BF16)) * jnp.exp(cs_x[j]))
        st_ref[j] = st * jnp.exp(cs_last) + _dot_tn(
            bc16[:, n_lanes[j]], (xdt * jnp.exp(cs_last - cs_x[j])).astype(BF16))
        cbs.append(_dot_nt(cc16[:, n_lanes[j]], bc16[:, n_lanes[j]]))
        xdt16s.append(xdt.astype(BF16))

    parts = [[] for _ in groups]
    for r in range(R):
        for j in groups:
            col = cs_w[j][:, r * L:(r + 1) * L]
            seg = col - col.T
            decay = jnp.where(causal, jnp.exp(jnp.where(causal, seg, 0.0)), 0.0)
            parts[j].append(_dot((cbs[j] * decay).astype(BF16),
                                 xdt16s[j][:, r * P:(r + 1) * P]))

    for j in groups:
        ln = g_lanes[j]
        y = ys[j] + jnp.concatenate(parts[j], axis=1) + dskip_ref[:, ln] * xc[:, ln]
        yz = y * _silu(z_ref[:, ln].astype(F32))
        o_ref[:, ln] = _rms_rows(yz, norm_ref[:, ln]).astype(o_ref.dtype)


def ssd_mix(pb, dt_raw, conv_w, conv_b, dt_bias, a_log, d_skip, b_norm,
            batch, seq, d_model):
    L, G, N = SSD_CHUNK, SSD_GROUPS, SSD_STATE
    n_heads = d_model // SSD_HEAD_DIM
    R = n_heads // G
    gw = d_model // G
    steps = seq // L
    m = batch * seq
    halo = 8
    gs = SSD_GROUPS_PER_STEP
    gsw, gsn = gs * gw, gs * N
    bn0 = 5 * d_model // gsn
    cwb0 = d_model // gsn

    def rep_heads(v):
        return jnp.repeat(v.astype(F32), SSD_HEAD_DIM).reshape(1, d_model)

    def pad_heads(v):
        return jnp.pad(v.astype(F32), (0, LANES - n_heads)).reshape(1, LANES)

    def select(width):
        row = jnp.arange(LANES, dtype=jnp.int32)[None, :, None]
        lane = jnp.arange(R * width, dtype=jnp.int32)[None, None, :]
        grp = jnp.arange(G, dtype=jnp.int32)[:, None, None]
        hit = (row % n_heads == grp * R + lane // width) & (row < 3 * n_heads)
        return hit.astype(BF16)

    row_g = lambda off: pl.BlockSpec((L, gsw), lambda b, g, s, off=off: (b * steps + s, off + g))
    row_n = lambda off: pl.BlockSpec((L, gsn), lambda b, g, s, off=off: (b * steps + s, off + g))
    par_g = lambda rows: pl.BlockSpec((rows, gsw), lambda b, g, s: (0, g))
    par_n = lambda rows, off: pl.BlockSpec((rows, gsn), lambda b, g, s, off=off: (0, off + g))
    head_vec = pl.BlockSpec((1, LANES), lambda b, g, s: (0, 0))
    sel = lambda width: pl.BlockSpec((gs, LANES, R * width), lambda b, g, s: (g, 0, 0))
    cb2 = conv_b.reshape(1, -1)
    n_steps_g = G // gs

    return pl.pallas_call(
        functools.partial(_ssd_kernel, heads_per_group=R, n_groups=gs, n_heads=n_heads),
        grid=(batch, n_steps_g, steps),
        in_specs=[row_g(4 * n_steps_g), row_n(bn0), row_n(bn0 + n_steps_g), row_g(3 * n_steps_g),
                  pl.BlockSpec((L, LANES), lambda b, g, s: (b * steps + s, 0)),
                  par_g(SSD_CONV), par_n(SSD_CONV, cwb0), par_n(SSD_CONV, cwb0 + n_steps_g),
                  par_g(1), par_n(1, cwb0), par_n(1, cwb0 + n_steps_g),
                  head_vec, head_vec, sel(SSD_HEAD_DIM), sel(L), par_g(1), par_g(1)],
        out_specs=pl.BlockSpec((L, gsw), lambda b, g, s: (b * steps + s, g)),
        out_shape=jax.ShapeDtypeStruct((m, d_model), BF16),
        scratch_shapes=[pltpu.VMEM((gs, N, gw), F32),
                        pltpu.VMEM((gsw // LANES, halo, LANES), F32),
                        pltpu.VMEM((gsn // LANES, halo, LANES), F32),
                        pltpu.VMEM((gsn // LANES, halo, LANES), F32),
                        pltpu.VMEM((gsw // LANES, halo + L, LANES), F32),
                        pltpu.VMEM((gsn // LANES, halo + L, LANES), F32),
                        pltpu.VMEM((gsn // LANES, halo + L, LANES), F32)],
        compiler_params=_cparams(("arbitrary", "arbitrary", "arbitrary")),
        name="ssd",
    )(pb, pb, pb, pb, dt_raw,
      conv_w, conv_w, conv_w, cb2, cb2, cb2,
      pad_heads(dt_bias), pad_heads(a_log),
      select(SSD_HEAD_DIM), select(L), rep_heads(d_skip), b_norm.reshape(1, d_model))


def _conf_tail_kernel(c_ref, dww_ref, dwb_ref, lng_ref, lnb_ref, w2_hbm, b2_ref,
                      h_ref, gpost_ref, gnext_ref, hout_ref, unext_ref,
                      carry_ref, ext_ref, conv_ref, w2_ref, stage_ref, sem_ref,
                      *, row_block, layer):
    tm, d = c_ref.shape
    halo = CONF_HALO
    first = halo - (CONF_KERNEL - 1)
    n_tiles = d // LANES

    @pl.when((pl.program_id(0) == 0) & (pl.program_id(1) == 0))
    def _():
        _load_weight_bf16(w2_hbm, layer, w2_ref, stage_ref, sem_ref)

    @pl.when(pl.program_id(1) == 0)
    def _():
        carry_ref[...] = jnp.zeros_like(carry_ref)

    for j in range(n_tiles):
        ln = slice(j * LANES, (j + 1) * LANES)
        ext_ref[j, 0:halo, :] = carry_ref[j]
        ext_ref[j, halo:halo + tm, :] = c_ref[:, ln]
        carry_ref[j] = c_ref[tm - halo:tm, ln]

    def col_tile(j, carry):
        cj = pl.ds(pl.multiple_of(j * LANES, LANES), LANES)
        for rb in range(tm // row_block):
            base = rb * row_block
            acc = dwb_ref[:, cj] + dww_ref[0:1, cj] * ext_ref[j, pl.ds(base + first, row_block), :]
            for k in range(1, CONF_KERNEL):
                acc = acc + dww_ref[k:k + 1, cj] * ext_ref[j, pl.ds(base + first + k, row_block), :]
            conv_ref[base:base + row_block, cj] = acc
        return carry

    lax.fori_loop(0, n_tiles, col_tile, 0)

    x = conv_ref[...]
    mu = jnp.mean(x, axis=-1, keepdims=True)
    xc = x - mu
    var = jnp.mean(xc * xc, axis=-1, keepdims=True)
    y = _silu(xc * lax.rsqrt(var + LN_EPS) * lng_ref[...] + lnb_ref[...])
    m = _dot(y.astype(BF16), w2_ref[...]) + b2_ref[...]
    _residual_epilogue(m, h_ref[...], gpost_ref[...], gnext_ref[...], hout_ref, unext_ref)


def conf_tail(c, dw_w, dw_b, ln_g, ln_b, w2_stack, layer, b2, h, gpost, gnext,
              batch, seq, tm=256, row_block=64, stage_rows=WEIGHT_STAGE_ROWS):
    m, d = c.shape
    tm = min(tm, seq)
    row_block = min(row_block, tm)
    stage_rows = min(stage_rows, d)
    steps = seq // tm
    row = pl.BlockSpec((tm, d), lambda b, s: (b * steps + s, 0))
    vec = pl.BlockSpec((1, d), lambda b, s: (0, 0))
    dww = jnp.pad(dw_w, ((0, CONF_HALO - CONF_KERNEL), (0, 0)))
    return pl.pallas_call(
        functools.partial(_conf_tail_kernel, row_block=row_block, layer=layer),
        grid=(batch, steps),
        in_specs=[row, pl.BlockSpec((CONF_HALO, d), lambda b, s: (0, 0)), vec, vec, vec,
                  pl.BlockSpec(memory_space=pl.ANY),
                  vec, row, vec, vec],
        out_specs=[row, row],
        out_shape=[jax.ShapeDtypeStruct((m, d), F32), jax.ShapeDtypeStruct((m, d), BF16)],
        scratch_shapes=[pltpu.VMEM((d // LANES, CONF_HALO, LANES), F32),
                        pltpu.VMEM((d // LANES, CONF_HALO + tm, LANES), F32),
                        pltpu.VMEM((tm, d), F32),
                        pltpu.VMEM((d, d), BF16), pltpu.VMEM((2, stage_rows, d), F32),
                        pltpu.SemaphoreType.DMA((2,))],
        compiler_params=_cparams(("arbitrary", "arbitrary")),
        name="conf_tail",
    )(c, dww, dw_b.reshape(1, d), ln_g.reshape(1, d), ln_b.reshape(1, d), w2_stack,
      b2.reshape(1, d), h, gpost.reshape(1, d), gnext.reshape(1, d))


def kernel(x, mix_pre_g, mix_post_g, ffn_pre_g, ffn_post_g, hgrn_lb_logits, even_w_in,
           hgrn_norm_g, ssd_conv_w, ssd_conv_b, ssd_dt_bias, ssd_a_log, ssd_d, ssd_norm_g,
           even_w_out, conf_w1, conf_b1, conf_dw_w, conf_dw_b, conf_ln_g, conf_ln_b,
           conf_w2, conf_b2, ffn_w_gate, ffn_w_up, ffn_w_down):
    batch, seq, d = x.shape
    depth = mix_pre_g.shape[0]
    hidden = ffn_w_gate.shape[2]
    m = batch * seq
    tn = 512
    tn_in = min(1024, d)
    main_cols = 6 * d + 2 * SSD_GROUPS * SSD_STATE
    n_ssd_heads = d // SSD_HEAD_DIM
    f_blocks = d // tn_in

    lb_p = jax.nn.softmax(hgrn_lb_logits.astype(F32), axis=0)
    lower_bounds = jnp.cumsum(lb_p, axis=0) - lb_p[0]

    w_in_t = jnp.swapaxes(even_w_in, 1, 2)
    conf_b1_3d = conf_b1.reshape(conf_b1.shape[0], 1, -1)
    plus = lambda off: (lambda n: n + off)

    h = x.reshape(m, d)
    u = prenorm(h, mix_pre_g[0])
    for layer in range(depth):
        i = layer // 2
        if layer % 2 == 0:
            skip_f = lambda n: n + jnp.where(n >= f_blocks, f_blocks, 0)
            pb = wide_proj(u, [w_in_t], i, [skip_f], [], [], (main_cols - d) // tn_in,
                           _identity_epilogue, BF16, "even_in_proj", tn=tn_in,
                           w_is_transposed=True)
            f_pre = wide_proj(u, [w_in_t], i, [plus(f_blocks)], [], [], f_blocks,
                              _identity_epilogue, F32, "even_f_proj", tn=tn_in,
                              w_is_transposed=True)
            dt_raw = small_proj(u, w_in_t, i, main_cols, n_ssd_heads)
            o_a = hgrn_mix(pb, f_pre, lower_bounds[i], hgrn_norm_g[i], batch, seq, d)
            o_b = ssd_mix(pb, dt_raw, ssd_conv_w[i], ssd_conv_b[i], ssd_dt_bias[i],
                          ssd_a_log[i], ssd_d[i], ssd_norm_g[i], batch, seq, d)
            h, u = out_proj([o_a, o_b], even_w_out, i, h, mix_post_g[layer],
                            ffn_pre_g[layer], True, "even_out_proj")
        else:
            c = wide_proj(u, [conf_w1, conf_w1], i, [plus(0), plus(d // tn)],
                          [conf_b1_3d, conf_b1_3d], [plus(0), plus(d // tn)], d // tn,
                          _glu_epilogue, F32, "conf_glu", tn=tn)
            h, u = conf_tail(c, conf_dw_w[i], conf_dw_b[i], conf_ln_g[i], conf_ln_b[i],
                             conf_w2, i, conf_b2[i], h, mix_post_g[layer],
                             ffn_pre_g[layer], batch, seq)
        act = wide_proj(u, [ffn_w_gate, ffn_w_up], layer, [plus(0), plus(0)], [], [],
                        hidden // tn, _swiglu_epilogue, BF16, "ffn_in", tn=tn)
        last = layer == depth - 1
        gnext = mix_pre_g[layer] if last else mix_pre_g[layer + 1]
        h, u = out_proj([act], ffn_w_down, layer, h, ffn_post_g[layer], gnext, not last,
                        "ffn_out")
    return h.reshape(batch, seq, d)
```

```python
import functools

import jax
import jax.numpy as jnp
from jax import lax
from jax.experimental import pallas as pl
from jax.experimental.pallas import tpu as pltpu

F32 = jnp.float32
BF16 = jnp.bfloat16

RMS_EPS = 1e-6
LN_EPS = 1e-5
HGRN_F_MIN = 1e-6
HGRN_HEAD = 128
HGRN_CHUNK = 64
HGRN_SUB = 8
HGRN_HEADS_PER_PART = 4
HGRN_PARTS_PER_STEP = 4
SSD_HEAD_DIM = 64
SSD_GROUPS = 4
SSD_STATE = 128
SSD_CONV = 4
SSD_CHUNK = 128
SSD_GROUPS_PER_STEP = 4
CONF_KERNEL = 31
CONF_HALO = 32
WEIGHT_STAGE_ROWS = 512
LANES = 128
VMEM_LIMIT = 56 * 1024 * 1024


def _cparams(semantics):
    return pltpu.CompilerParams(dimension_semantics=semantics,
                                vmem_limit_bytes=VMEM_LIMIT)


def _sigmoid(x):
    return 0.5 * jnp.tanh(0.5 * x) + 0.5


def _silu(x):
    t = 0.5 * x
    return t * jnp.tanh(t) + t


def _softplus(x):
    return jnp.maximum(x, 0.0) + jnp.log1p(jnp.exp(-jnp.abs(x)))


def _rms_rows(x, g, eps=RMS_EPS):
    ms = jnp.mean(x * x, axis=-1, keepdims=True)
    return x * lax.rsqrt(ms + eps) * g


def _dot(a, b):
    return jnp.dot(a, b, preferred_element_type=F32)


def _dot_nt(a, b):
    return lax.dot_general(a, b, (((1,), (1,)), ((), ())), preferred_element_type=F32)


def _dot_tn(a, b):
    return lax.dot_general(a, b, (((0,), (0,)), ((), ())), preferred_element_type=F32)


def _split3(x):
    hi = x.astype(BF16)
    r1 = x - hi.astype(F32)
    mid = r1.astype(BF16)
    lo = (r1 - mid.astype(F32)).astype(BF16)
    return hi, mid, lo


def _dot_exact_rhs(sel, x):
    return _dot(jnp.concatenate([sel, sel, sel], axis=1),
                jnp.concatenate(_split3(x), axis=0))


def _lane_stack3(x, n):
    hi, mid, lo = _split3(x)
    return (hi.astype(F32) + pltpu.roll(mid.astype(F32), n, 1)
            + pltpu.roll(lo.astype(F32), 2 * n, 1)).astype(BF16)


def _lower_tri(n, dtype):
    r = lax.broadcasted_iota(jnp.int32, (n, n), 0)
    c = lax.broadcasted_iota(jnp.int32, (n, n), 1)
    return (r >= c).astype(dtype)


def _prenorm_kernel(x_ref, g_ref, o_ref):
    o_ref[...] = _rms_rows(x_ref[...], g_ref[...]).astype(o_ref.dtype)


def prenorm(x, g, tm=512):
    m, d = x.shape
    tm = min(tm, m)
    return pl.pallas_call(
        _prenorm_kernel,
        grid=(m // tm,),
        in_specs=[pl.BlockSpec((tm, d), lambda i: (i, 0)),
                  pl.BlockSpec((1, d), lambda i: (0, 0))],
        out_specs=pl.BlockSpec((tm, d), lambda i: (i, 0)),
        out_shape=jax.ShapeDtypeStruct((m, d), BF16),
        compiler_params=_cparams(("arbitrary",)),
        name="prenorm",
    )(x, g.reshape(1, d))


def _wide_kernel(*refs, n_w, n_b, epilogue, w_is_transposed):
    u_ref = refs[0]
    w_refs = refs[1:1 + n_w]
    b_refs = refs[1 + n_w:1 + n_w + n_b]
    o_ref = refs[1 + n_w + n_b]
    s_refs = refs[2 + n_w + n_b:]

    @pl.when(pl.program_id(1) == 0)
    def _():
        for w, s in zip(w_refs, s_refs):
            s[...] = (w[...].T if w_is_transposed else w[...]).astype(BF16)

    u = u_ref[...]
    ys = [_dot(u, s[...]) for s in s_refs]
    o_ref[...] = epilogue(*ys, *[b[...] for b in b_refs]).astype(o_ref.dtype)


def wide_proj(u, ws, layer, w_col_maps, bs, b_col_maps, n_blocks, epilogue, out_dtype,
              name, tn=512, tm=1024, w_is_transposed=False):
    m, k = u.shape
    tm = min(tm, m)
    in_specs = [pl.BlockSpec((tm, k), lambda n, i: (i, 0))]
    for cmap in w_col_maps:
        if w_is_transposed:
            in_specs.append(pl.BlockSpec((None, tn, k), lambda n, i, cmap=cmap: (layer, cmap(n), 0)))
        else:
            in_specs.append(pl.BlockSpec((None, k, tn), lambda n, i, cmap=cmap: (layer, 0, cmap(n))))
    for cmap in b_col_maps:
        in_specs.append(pl.BlockSpec((None, 1, tn), lambda n, i, cmap=cmap: (layer, 0, cmap(n))))
    return pl.pallas_call(
        functools.partial(_wide_kernel, n_w=len(ws), n_b=len(bs), epilogue=epilogue,
                          w_is_transposed=w_is_transposed),
        grid=(n_blocks, m // tm),
        in_specs=in_specs,
        out_specs=pl.BlockSpec((tm, tn), lambda n, i: (i, n)),
        out_shape=jax.ShapeDtypeStruct((m, n_blocks * tn), out_dtype),
        scratch_shapes=[pltpu.VMEM((k, tn), BF16) for _ in ws],
        compiler_params=_cparams(("arbitrary", "arbitrary")),
        name=name,
    )(u, *ws, *bs)


def _identity_epilogue(y):
    return y


def _swiglu_epilogue(g, up):
    return _silu(g) * up


def _glu_epilogue(a, g, ba, bg):
    return (a + ba) * _sigmoid(g + bg)


def _small_proj_kernel(u_ref, w_ref, o_ref):
    n, k = w_ref.shape
    w = jnp.concatenate([w_ref[...], jnp.zeros((LANES - n, k), F32)], axis=0)
    o_ref[...] = _dot_nt(u_ref[...], w.astype(BF16))


def small_proj(u, w_t_stack, layer, first_row, n, tm=1024):
    m, k = u.shape
    tm = min(tm, m)
    return pl.pallas_call(
        _small_proj_kernel,
        grid=(m // tm,),
        in_specs=[pl.BlockSpec((tm, k), lambda i: (i, 0)),
                  pl.BlockSpec((None, n, k), lambda i: (layer, first_row // n, 0))],
        out_specs=pl.BlockSpec((tm, LANES), lambda i: (i, 0)),
        out_shape=jax.ShapeDtypeStruct((m, LANES), F32),
        compiler_params=_cparams(("arbitrary",)),
        name="dt_proj",
    )(u, w_t_stack)


def _residual_epilogue(m, h, gpost, gnext, hout_ref, unext_ref):
    hn = h + _rms_rows(m, gpost)
    hout_ref[...] = hn
    if unext_ref is not None:
        unext_ref[...] = _rms_rows(hn, gnext).astype(unext_ref.dtype)


def _load_weight_bf16(w_hbm, layer, w16_ref, stage_ref, sem_ref):
    rows = stage_ref.shape[1]
    n_chunks = w16_ref.shape[0] // rows

    def chunk_copy(c):
        return pltpu.make_async_copy(w_hbm.at[layer, pl.ds(c * rows, rows), :],
                                     stage_ref.at[c % 2], sem_ref.at[c % 2])

    chunk_copy(0).start()
    for c in range(n_chunks):
        if c + 1 < n_chunks:
            chunk_copy(c + 1).start()
        chunk_copy(c).wait()
        w16_ref[c * rows:(c + 1) * rows, :] = stage_ref[c % 2].astype(BF16)


def _out_proj_kernel(*refs, n_a, with_next, layer):
    a_refs = refs[:n_a]
    w_hbm, h_ref, gpost_ref, gnext_ref, hout_ref = refs[n_a:n_a + 5]
    unext_ref = refs[n_a + 5] if with_next else None
    w16_ref, stage_ref, sem_ref = refs[-3:]

    @pl.when(pl.program_id(0) == 0)
    def _():
        _load_weight_bf16(w_hbm, layer, w16_ref, stage_ref, sem_ref)

    off = 0
    m = None
    for a_ref in a_refs:
        ka = a_ref.shape[1]
        part = _dot(a_ref[...], w16_ref[off:off + ka, :])
        m = part if m is None else m + part
        off += ka
    _residual_epilogue(m, h_ref[...], gpost_ref[...], gnext_ref[...], hout_ref, unext_ref)


def out_proj(acts, w_stack, layer, h, gpost, gnext, with_next, name, tm=256,
             stage_rows=WEIGHT_STAGE_ROWS):
    m = h.shape[0]
    _, kk, d = w_stack.shape
    tm = min(tm, m)
    stage_rows = min(stage_rows, kk)
    row = pl.BlockSpec((tm, d), lambda i: (i, 0))
    vec = pl.BlockSpec((1, d), lambda i: (0, 0))
    out_shape = [jax.ShapeDtypeStruct((m, d), F32)]
    out_specs = [row]
    if with_next:
        out_shape.append(jax.ShapeDtypeStruct((m, d), BF16))
        out_specs.append(row)
    res = pl.pallas_call(
        functools.partial(_out_proj_kernel, n_a=len(acts), with_next=with_next, layer=layer),
        grid=(m // tm,),
        in_specs=[pl.BlockSpec((tm, a.shape[1]), lambda i: (i, 0)) for a in acts] + [
            pl.BlockSpec(memory_space=pl.ANY), row, vec, vec],
        out_specs=out_specs,
        out_shape=out_shape,
        scratch_shapes=[pltpu.VMEM((kk, d), BF16), pltpu.VMEM((2, stage_rows, d), F32),
                        pltpu.SemaphoreType.DMA((2,))],
        compiler_params=_cparams(("arbitrary",)),
        name=name,
    )(*acts, w_stack, h, gpost.reshape(1, d), gnext.reshape(1, d))
    return (res[0], res[1]) if with_next else (res[0], None)


def _hgrn_kernel(q_ref, f_ref, v_ref, gate_ref, lb_ref, an_ref, o_ref, st_ref,
                 *, n_chunks, n_heads, part_heads):
    C, SUB, HD = HGRN_CHUNK, HGRN_SUB, HGRN_HEAD
    n_sub = C // SUB
    mid = SUB // 2 - 1
    width = part_heads * HD
    head_lanes = [slice(h * HD, (h + 1) * HD) for h in range(part_heads)]
    parts = [slice(p * width, (p + 1) * width) for p in range(n_heads // part_heads)]

    @pl.when(pl.program_id(2) == 0)
    def _():
        st_ref[...] = jnp.zeros_like(st_ref)

    tri = _lower_tri(C, BF16)
    causal = (lax.broadcasted_iota(jnp.int32, (C, C), 0)
              >= lax.broadcasted_iota(jnp.int32, (C, C), 1))
    def gates(sl, part):
        lb = lb_ref[:, part]
        sig = _sigmoid(f_ref[sl, part])
        f = lb + (1.0 - lb) * sig
        k = 1.0 - f
        log_f = jnp.log(jnp.maximum(f, HGRN_F_MIN))
        qs = _silu(q_ref[sl, part].astype(F32))
        b = _dot_exact_rhs(tri, log_f)
        return qs, k, b, v_ref[sl, part]

    def scores_and_state(vals, h0):
        qs, k, b, v16 = vals
        b_last = b[C - 1:C, :]

        q_in16 = (qs * jnp.exp(b)).astype(BF16)
        k_end16 = (k * jnp.exp(b_last - b)).astype(BF16)
        dec_last = jnp.exp(b_last)
        sts = [st_ref[h0 + h] for h in range(part_heads)]
        o_heads = [_dot(q_in16[:, s], st.astype(BF16)) for s, st in zip(head_lanes, sts)]
        for h, (s, st) in enumerate(zip(head_lanes, sts)):
            row_decay = jnp.broadcast_to(dec_last[:, s], (HD, HD)).T
            st_ref[h0 + h] = st * row_decay + _dot_tn(k_end16[:, s], v16[:, s])

        blocks = [[] for _ in range(part_heads)]
        keys = []
        prev_ref = None
        for i in range(n_sub):
            rows = slice(i * SUB, (i + 1) * SUB)
            ref_b = b[i * SUB + mid:i * SUB + mid + 1, :]
            if keys:
                step = jnp.exp(ref_b - prev_ref)
                keys = [kj * step for kj in keys]
            keys.append(k[rows, :] * jnp.exp(ref_b - b[rows, :]))
            prev_ref = ref_b
            qt16 = (qs[rows, :] * jnp.exp(b[rows, :] - ref_b)).astype(BF16)
            pad = [jnp.zeros(((n_sub - 1 - i) * SUB, width), F32)] if i < n_sub - 1 else []
            kt16 = jnp.concatenate(keys + pad, axis=0).astype(BF16)
            for h, s in enumerate(head_lanes):
                blocks[h].append(_dot_nt(qt16[:, s], kt16[:, s]))
        return o_heads, blocks

    def outputs(sl, part, vals, o_heads, blocks):
        v16 = vals[3]
        for h, s in enumerate(head_lanes):
            scores = jnp.where(causal, jnp.concatenate(blocks[h], axis=0), 0.0)
            o_heads[h] = o_heads[h] + _dot(scores.astype(BF16), v16[:, s])
        y = jnp.concatenate(
            [o * lax.rsqrt(jnp.mean(o * o, axis=-1, keepdims=True) + RMS_EPS) for o in o_heads],
            axis=1)
        y = y * an_ref[:, part] * _silu(gate_ref[sl, part].astype(F32))
        o_ref[sl, part] = y.astype(o_ref.dtype)

    def chunk(c, carry):
        sl = pl.ds(pl.multiple_of(c * C, C), C)
        vals = [gates(sl, part) for part in parts]
        mids = {}
        for p in range(len(parts) + 1):
            if p < len(parts):
                mids[p] = scores_and_state(vals[p], p * part_heads)
            if p >= 1:
                outputs(sl, parts[p - 1], vals[p - 1], *mids[p - 1])
        return carry

    lax.fori_loop(0, n_chunks, chunk, 0)


def hgrn_mix(pb, f_pre, lb, a_norm, batch, seq, d_model, rows_per_step=512):
    n_heads = d_model // HGRN_HEAD
    part_heads = min(HGRN_HEADS_PER_PART, n_heads)
    hb = min(HGRN_PARTS_PER_STEP * part_heads, n_heads)
    gw = hb * HGRN_HEAD
    n_groups = n_heads // hb
    cs = min(rows_per_step, seq)
    steps = seq // cs
    m = batch * seq

    def col(seg):
        return pl.BlockSpec((cs, gw), lambda b, h, s, seg=seg: (b * steps + s, seg * n_groups + h))

    vec = pl.BlockSpec((1, gw), lambda b, h, s: (0, h))
    return pl.pallas_call(
        functools.partial(_hgrn_kernel, n_chunks=cs // HGRN_CHUNK, n_heads=hb,
                          part_heads=part_heads),
        grid=(batch, n_groups, steps),
        in_specs=[col(0), col(0), col(1), col(2), vec, vec],
        out_specs=col(0),
        out_shape=jax.ShapeDtypeStruct((m, d_model), BF16),
        scratch_shapes=[pltpu.VMEM((hb, HGRN_HEAD, HGRN_HEAD), F32)],
        compiler_params=_cparams(("arbitrary", "arbitrary", "arbitrary")),
        name="hgrn2",
    )(pb, f_pre, pb, pb, lb.reshape(1, d_model), a_norm.reshape(1, d_model))


def _causal_conv(raw_ref, carry_ref, ext_ref, w_ref, b_ref, n_rows, n_taps, halo):
    first = halo - n_taps + 1
    outs = []
    for j in range(raw_ref.shape[1] // LANES):
        ln = slice(j * LANES, (j + 1) * LANES)
        raw = raw_ref[:, ln].astype(F32)
        ext_ref[j, 0:halo, :] = carry_ref[j]
        ext_ref[j, halo:halo + n_rows, :] = raw
        carry_ref[j] = raw[n_rows - halo:n_rows, :]
        acc = b_ref[:, ln] + w_ref[0:1, ln] * ext_ref[j, pl.ds(first, n_rows), :]
        for k in range(1, n_taps):
            acc = acc + w_ref[k:k + 1, ln] * ext_ref[j, pl.ds(first + k, n_rows), :]
        outs.append(acc)
    return jnp.concatenate(outs, axis=1)


def _ssd_kernel(x_ref, bm_ref, cm_ref, z_ref, dt_ref,
                cwx_ref, cwb_ref, cwc_ref, cbx_ref, cbb_ref, cbc_ref,
                dtb_ref, alog_ref, selp_ref, selw_ref, dskip_ref, norm_ref, o_ref,
                st_ref, carx_ref, carb_ref, carc_ref, extx_ref, extb_ref, extc_ref,
                *, heads_per_group, n_groups, n_heads):
    L, P, R, N = SSD_CHUNK, SSD_HEAD_DIM, heads_per_group, SSD_STATE
    gw = R * P
    groups = range(n_groups)
    g_lanes = [slice(j * gw, (j + 1) * gw) for j in groups]
    n_lanes = [slice(j * N, (j + 1) * N) for j in groups]

    @pl.when(pl.program_id(2) == 0)
    def _():
        st_ref[...] = jnp.zeros_like(st_ref)
        carx_ref[...] = jnp.zeros_like(carx_ref)
        carb_ref[...] = jnp.zeros_like(carb_ref)
        carc_ref[...] = jnp.zeros_like(carc_ref)

    halo = carx_ref.shape[1]
    xc = _silu(_causal_conv(x_ref, carx_ref, extx_ref, cwx_ref, cbx_ref, L, SSD_CONV, halo))
    bc16 = _silu(_causal_conv(bm_ref, carb_ref, extb_ref, cwb_ref, cbb_ref, L, SSD_CONV,
                              halo)).astype(BF16)
    cc16 = _silu(_causal_conv(cm_ref, carc_ref, extc_ref, cwc_ref, cbc_ref, L, SSD_CONV,
                              halo)).astype(BF16)

    lane = lax.broadcasted_iota(jnp.int32, (L, LANES), 1)
    dt = jnp.where(lane < n_heads, _softplus(dt_ref[...] + dtb_ref[...]), 0.0)
    dta = dt * (-jnp.exp(alog_ref[...]))
    cs = _dot_exact_rhs(_lower_tri(L, BF16), dta)

    dt3 = _lane_stack3(dt, n_heads)
    cs3 = _lane_stack3(cs, n_heads)
    dt_x = [_dot(dt3, selp_ref[j]) for j in groups]
    cs_x = [_dot(cs3, selp_ref[j]) for j in groups]
    cs_w = [_dot(cs3, selw_ref[j]) for j in groups]

    rr = lax.broadcasted_iota(jnp.int32, (L, L), 0)
    cl = lax.broadcasted_iota(jnp.int32, (L, L), 1)
    causal = rr >= cl
    ys, xdt16s, cbs = [], [], []
    for j in groups:
        xdt = xc[:, g_lanes[j]] * dt_x[j]
        cs_last = cs_x[j][L - 1:L, :]
        st = st_ref[j]
        ys.append(_dot(cc16[:, n_lanes[j]], st.astype(BF16)) * jnp.exp(cs_x[j]))
        st_ref[j] = st * jnp.exp(cs_last) + _dot_tn(
            bc16[:, n_lanes[j]], (xdt * jnp.exp(cs_last - cs_x[j])).astype(BF16))
        cbs.append(_dot_nt(cc16[:, n_lanes[j]], bc16[:, n_lanes[j]]))
        xdt16s.append(xdt.astype(BF16))

    parts = [[] for _ in groups]
    for r in range(R):
        for j in groups:
            col = cs_w[j][:, r * L:(r + 1) * L]
            seg = col - col.T
            decay = jnp.where(causal, jnp.exp(jnp.where(causal, seg, 0.0)), 0.0)
            parts[j].append(_dot((cbs[j] * decay).astype(BF16),
                                 xdt16s[j][:, r * P:(r + 1) * P]))

    for j in groups:
        ln = g_lanes[j]
        y = ys[j] + jnp.concatenate(parts[j], axis=1) + dskip_ref[:, ln] * xc[:, ln]
        yz = y * _silu(z_ref[:, ln].astype(F32))
        o_ref[:, ln] = _rms_rows(yz, norm_ref[:, ln]).astype(o_ref.dtype)


def ssd_mix(pb, dt_raw, conv_w, conv_b, dt_bias, a_log, d_skip, b_norm,
            batch, seq, d_model):
    L, G, N = SSD_CHUNK, SSD_GROUPS, SSD_STATE
    n_heads = d_model // SSD_HEAD_DIM
    R = n_heads // G
    gw = d_model // G
    steps = seq // L
    m = batch * seq
    halo = 8
    gs = SSD_GROUPS_PER_STEP
    gsw, gsn = gs * gw, gs * N
    bn0 = 5 * d_model // gsn
    cwb0 = d_model // gsn

    def rep_heads(v):
        return jnp.repeat(v.astype(F32), SSD_HEAD_DIM).reshape(1, d_model)

    def pad_heads(v):
        return jnp.pad(v.astype(F32), (0, LANES - n_heads)).reshape(1, LANES)

    def select(width):
        row = jnp.arange(LANES, dtype=jnp.int32)[None, :, None]
        lane = jnp.arange(R * width, dtype=jnp.int32)[None, None, :]
        grp = jnp.arange(G, dtype=jnp.int32)[:, None, None]
        hit = (row % n_heads == grp * R + lane // width) & (row < 3 * n_heads)
        return hit.astype(BF16)

    row_g = lambda off: pl.BlockSpec((L, gsw), lambda b, g, s, off=off: (b * steps + s, off + g))
    row_n = lambda off: pl.BlockSpec((L, gsn), lambda b, g, s, off=off: (b * steps + s, off + g))
    par_g = lambda rows: pl.BlockSpec((rows, gsw), lambda b, g, s: (0, g))
    par_n = lambda rows, off: pl.BlockSpec((rows, gsn), lambda b, g, s, off=off: (0, off + g))
    head_vec = pl.BlockSpec((1, LANES), lambda b, g, s: (0, 0))
    sel = lambda width: pl.BlockSpec((gs, LANES, R * width), lambda b, g, s: (g, 0, 0))
    cb2 = conv_b.reshape(1, -1)
    n_steps_g = G // gs

    return pl.pallas_call(
        functools.partial(_ssd_kernel, heads_per_group=R, n_groups=gs, n_heads=n_heads),
        grid=(batch, n_steps_g, steps),
        in_specs=[row_g(4 * n_steps_g), row_n(bn0), row_n(bn0 + n_steps_g), row_g(3 * n_steps_g),
                  pl.BlockSpec((L, LANES), lambda b, g, s: (b * steps + s, 0)),
                  par_g(SSD_CONV), par_n(SSD_CONV, cwb0), par_n(SSD_CONV, cwb0 + n_steps_g),
                  par_g(1), par_n(1, cwb0), par_n(1, cwb0 + n_steps_g),
                  head_vec, head_vec, sel(SSD_HEAD_DIM), sel(L), par_g(1), par_g(1)],
        out_specs=pl.BlockSpec((L, gsw), lambda b, g, s: (b * steps + s, g)),
        out_shape=jax.ShapeDtypeStruct((m, d_model), BF16),
        scratch_shapes=[pltpu.VMEM((gs, N, gw), F32),
                        pltpu.VMEM((gsw // LANES, halo, LANES), F32),
                        pltpu.VMEM((gsn // LANES, halo, LANES), F32),
                        pltpu.VMEM((gsn // LANES, halo, LANES), F32),
                        pltpu.VMEM((gsw // LANES, halo + L, LANES), F32),
                        pltpu.VMEM((gsn // LANES, halo + L, LANES), F32),
                        pltpu.VMEM((gsn // LANES, halo + L, LANES), F32)],
        compiler_params=_cparams(("arbitrary", "arbitrary", "arbitrary")),
        name="ssd",
    )(pb, pb, pb, pb, dt_raw,
      conv_w, conv_w, conv_w, cb2, cb2, cb2,
      pad_heads(dt_bias), pad_heads(a_log),
      select(SSD_HEAD_DIM), select(L), rep_heads(d_skip), b_norm.reshape(1, d_model))


def _conf_tail_kernel(c_ref, dww_ref, dwb_ref, lng_ref, lnb_ref, w2_hbm, b2_ref,
                      h_ref, gpost_ref, gnext_ref, hout_ref, unext_ref,
                      carry_ref, ext_ref, conv_ref, w2_ref, stage_ref, sem_ref,
                      *, row_block, layer):
    tm, d = c_ref.shape
    halo = CONF_HALO
    first = halo - (CONF_KERNEL - 1)
    n_tiles = d // LANES

    @pl.when((pl.program_id(0) == 0) & (pl.program_id(1) == 0))
    def _():
        _load_weight_bf16(w2_hbm, layer, w2_ref, stage_ref, sem_ref)

    @pl.when(pl.program_id(1) == 0)
    def _():
        carry_ref[...] = jnp.zeros_like(carry_ref)

    for j in range(n_tiles):
        ln = slice(j * LANES, (j + 1) * LANES)
        ext_ref[j, 0:halo, :] = carry_ref[j]
        ext_ref[j, halo:halo + tm, :] = c_ref[:, ln]
        carry_ref[j] = c_ref[tm - halo:tm, ln]

    def col_tile(j, carry):
        cj = pl.ds(pl.multiple_of(j * LANES, LANES), LANES)
        for rb in range(tm // row_block):
            base = rb * row_block
            acc = dwb_ref[:, cj] + dww_ref[0:1, cj] * ext_ref[j, pl.ds(base + first, row_block), :]
            for k in range(1, CONF_KERNEL):
                acc = acc + dww_ref[k:k + 1, cj] * ext_ref[j, pl.ds(base + first + k, row_block), :]
            conv_ref[base:base + row_block, cj] = acc
        return carry

    lax.fori_loop(0, n_tiles, col_tile, 0)

    x = conv_ref[...]
    mu = jnp.mean(x, axis=-1, keepdims=True)
    xc = x - mu
    var = jnp.mean(xc * xc, axis=-1, keepdims=True)
    y = _silu(xc * lax.rsqrt(var + LN_EPS) * lng_ref[...] + lnb_ref[...])
    m = _dot(y.astype(BF16), w2_ref[...]) + b2_ref[...]
    _residual_epilogue(m, h_ref[...], gpost_ref[...], gnext_ref[...], hout_ref, unext_ref)


def conf_tail(c, dw_w, dw_b, ln_g, ln_b, w2_stack, layer, b2, h, gpost, gnext,
              batch, seq, tm=256, row_block=64, stage_rows=WEIGHT_STAGE_ROWS):
    m, d = c.shape
    tm = min(tm, seq)
    row_block = min(row_block, tm)
    stage_rows = min(stage_rows, d)
    steps = seq // tm
    row = pl.BlockSpec((tm, d), lambda b, s: (b * steps + s, 0))
    vec = pl.BlockSpec((1, d), lambda b, s: (0, 0))
    dww = jnp.pad(dw_w, ((0, CONF_HALO - CONF_KERNEL), (0, 0)))
    return pl.pallas_call(
        functools.partial(_conf_tail_kernel, row_block=row_block, layer=layer),
        grid=(batch, steps),
        in_specs=[row, pl.BlockSpec((CONF_HALO, d), lambda b, s: (0, 0)), vec, vec, vec,
                  pl.BlockSpec(memory_space=pl.ANY),
                  vec, row, vec, vec],
        out_specs=[row, row],
        out_shape=[jax.ShapeDtypeStruct((m, d), F32), jax.ShapeDtypeStruct((m, d), BF16)],
        scratch_shapes=[pltpu.VMEM((d // LANES, CONF_HALO, LANES), F32),
                        pltpu.VMEM((d // LANES, CONF_HALO + tm, LANES), F32),
                        pltpu.VMEM((tm, d), F32),
                        pltpu.VMEM((d, d), BF16), pltpu.VMEM((2, stage_rows, d), F32),
                        pltpu.SemaphoreType.DMA((2,))],
        compiler_params=_cparams(("arbitrary", "arbitrary")),
        name="conf_tail",
    )(c, dww, dw_b.reshape(1, d), ln_g.reshape(1, d), ln_b.reshape(1, d), w2_stack,
      b2.reshape(1, d), h, gpost.reshape(1, d), gnext.reshape(1, d))


def kernel(x, mix_pre_g, mix_post_g, ffn_pre_g, ffn_post_g, hgrn_lb_logits, even_w_in,
           hgrn_norm_g, ssd_conv_w, ssd_conv_b, ssd_dt_bias, ssd_a_log, ssd_d, ssd_norm_g,
           even_w_out, conf_w1, conf_b1, conf_dw_w, conf_dw_b, conf_ln_g, conf_ln_b,
           conf_w2, conf_b2, ffn_w_gate, ffn_w_up, ffn_w_down):
    batch, seq, d = x.shape
    depth = mix_pre_g.shape[0]
    hidden = ffn_w_gate.shape[2]
    m = batch * seq
    tn = 512
    tn_in = min(1024, d)
    main_cols = 6 * d + 2 * SSD_GROUPS * SSD_STATE
    n_ssd_heads = d // SSD_HEAD_DIM
    f_blocks = d // tn_in

    lb_p = jax.nn.softmax(hgrn_lb_logits.astype(F32), axis=0)
    lower_bounds = jnp.cumsum(lb_p, axis=0) - lb_p[0]

    w_in_t = jnp.swapaxes(even_w_in, 1, 2)
    conf_b1_3d = conf_b1.reshape(conf_b1.shape[0], 1, -1)
    plus = lambda off: (lambda n: n + off)

    h = x.reshape(m, d)
    u = prenorm(h, mix_pre_g[0])
    for layer in range(depth):
        i = layer // 2
        if layer % 2 == 0:
            skip_f = lambda n: n + jnp.where(n >= f_blocks, f_blocks, 0)
            pb = wide_proj(u, [w_in_t], i, [skip_f], [], [], (main_cols - d) // tn_in,
                           _identity_epilogue, BF16, "even_in_proj", tn=tn_in,
                           w_is_transposed=True)
            f_pre = wide_proj(u, [w_in_t], i, [plus(f_blocks)], [], [], f_blocks,
                              _identity_epilogue, F32, "even_f_proj", tn=tn_in,
                              w_is_transposed=True)
            dt_raw = small_proj(u, w_in_t, i, main_cols, n_ssd_heads)
            o_a = hgrn_mix(pb, f_pre, lower_bounds[i], hgrn_norm_g[i], batch, seq, d)
            o_b = ssd_mix(pb, dt_raw, ssd_conv_w[i], ssd_conv_b[i], ssd_dt_bias[i],
                          ssd_a_log[i], ssd_d[i], ssd_norm_g[i], batch, seq, d)
            h, u = out_proj([o_a, o_b], even_w_out, i, h, mix_post_g[layer],
                            ffn_pre_g[layer], True, "even_out_proj")
        else:
            c = wide_proj(u, [conf_w1, conf_w1], i, [plus(0), plus(d // tn)],
                          [conf_b1_3d, conf_b1_3d], [plus(0), plus(d // tn)], d // tn,
                          _glu_epilogue, F32, "conf_glu", tn=tn)
            h, u = conf_tail(c, conf_dw_w[i], conf_dw_b[i], conf_ln_g[i], conf_ln_b[i],
                             conf_w2, i, conf_b2[i], h, mix_post_g[layer],
                             ffn_pre_g[layer], batch, seq)
        act = wide_proj(u, [ffn_w_gate, ffn_w_up], layer, [plus(0), plus(0)], [], [],
                        hidden // tn, _swiglu_epilogue, BF16, "ffn_in", tn=tn)
        last = layer == depth - 1
        gnext = mix_pre_g[layer] if last else mix_pre_g[layer + 1]
        h, u = out_proj([act], ffn_w_down, layer, h, ffn_post_g[layer], gnext, not last,
                        "ffn_out")
    return h.reshape(batch, seq, d)
```

```python
import functools

import jax
import jax.numpy as jnp
from jax import lax
from jax.experimental import pallas as pl
from jax.experimental.pallas import tpu as pltpu

F32 = jnp.float32
BF16 = jnp.bfloat16

LOG2_E = 1.4426950408889634
RMS_EPS = 1e-6
LN_EPS = 1e-5
HGRN_F_MIN = 1e-6
HGRN_HEAD = 128
HGRN_CHUNK = 64
HGRN_SUB = 8
HGRN_HEADS_PER_PART = 4
HGRN_PARTS_PER_STEP = 4
SSD_HEAD_DIM = 64
SSD_GROUPS = 4
SSD_STATE = 128
SSD_CONV = 4
SSD_CHUNK = 128
SSD_GROUPS_PER_STEP = 4
CONF_KERNEL = 31
CONF_HALO = 32
WEIGHT_STAGE_ROWS = 512
LANES = 128
VMEM_LIMIT = 56 * 1024 * 1024


def _cparams(semantics):
    return pltpu.CompilerParams(dimension_semantics=semantics,
                                vmem_limit_bytes=VMEM_LIMIT)


def _sigmoid(x):
    return 0.5 * jnp.tanh(0.5 * x) + 0.5


def _silu(x):
    t = 0.5 * x
    return t * jnp.tanh(t) + t


def _softplus(x):
    return jnp.maximum(x, 0.0) + jnp.log1p(jnp.exp(-jnp.abs(x)))


def _rms_rows(x, g, eps=RMS_EPS):
    ms = jnp.mean(x * x, axis=-1, keepdims=True)
    return x * lax.rsqrt(ms + eps) * g


def _dot(a, b):
    return jnp.dot(a, b, preferred_element_type=F32)


def _dot_nt(a, b):
    return lax.dot_general(a, b, (((1,), (1,)), ((), ())), preferred_element_type=F32)


def _dot_tn(a, b):
    return lax.dot_general(a, b, (((0,), (0,)), ((), ())), preferred_element_type=F32)


def _split3(x):
    hi = x.astype(BF16)
    r1 = x - hi.astype(F32)
    mid = r1.astype(BF16)
    lo = (r1 - mid.astype(F32)).astype(BF16)
    return hi, mid, lo


def _dot_exact_rhs(sel, x):
    return _dot(jnp.concatenate([sel, sel, sel], axis=1),
                jnp.concatenate(_split3(x), axis=0))


def _lane_stack3(x, n):
    hi, mid, lo = _split3(x)
    return (hi.astype(F32) + pltpu.roll(mid.astype(F32), n, 1)
            + pltpu.roll(lo.astype(F32), 2 * n, 1)).astype(BF16)


def _lower_tri(n, dtype):
    r = lax.broadcasted_iota(jnp.int32, (n, n), 0)
    c = lax.broadcasted_iota(jnp.int32, (n, n), 1)
    return (r >= c).astype(dtype)


def _prenorm_kernel(x_ref, g_ref, o_ref):
    o_ref[...] = _rms_rows(x_ref[...], g_ref[...]).astype(o_ref.dtype)


def prenorm(x, g, tm=512):
    m, d = x.shape
    tm = min(tm, m)
    return pl.pallas_call(
        _prenorm_kernel,
        grid=(m // tm,),
        in_specs=[pl.BlockSpec((tm, d), lambda i: (i, 0)),
                  pl.BlockSpec((1, d), lambda i: (0, 0))],
        out_specs=pl.BlockSpec((tm, d), lambda i: (i, 0)),
        out_shape=jax.ShapeDtypeStruct((m, d), BF16),
        compiler_params=_cparams(("arbitrary",)),
        name="prenorm",
    )(x, g.reshape(1, d))


def _wide_kernel(*refs, n_w, n_b, epilogue, w_is_transposed):
    u_ref = refs[0]
    w_refs = refs[1:1 + n_w]
    b_refs = refs[1 + n_w:1 + n_w + n_b]
    o_ref = refs[1 + n_w + n_b]
    s_refs = refs[2 + n_w + n_b:]

    @pl.when(pl.program_id(1) == 0)
    def _():
        for w, s in zip(w_refs, s_refs):
            s[...] = w[...].astype(BF16)

    u = u_ref[...]
    dot = _dot_nt if w_is_transposed else _dot
    ys = [dot(u, s[...]) for s in s_refs]
    o_ref[...] = epilogue(*ys, *[b[...] for b in b_refs]).astype(o_ref.dtype)


def wide_proj(u, ws, layer, w_col_maps, bs, b_col_maps, n_blocks, epilogue, out_dtype,
              name, tn=512, tm=1024, w_is_transposed=False):
    m, k = u.shape
    tm = min(tm, m)
    in_specs = [pl.BlockSpec((tm, k), lambda n, i: (i, 0))]
    for cmap in w_col_maps:
        if w_is_transposed:
            in_specs.append(pl.BlockSpec((None, tn, k), lambda n, i, cmap=cmap: (layer, cmap(n), 0)))
        else:
            in_specs.append(pl.BlockSpec((None, k, tn), lambda n, i, cmap=cmap: (layer, 0, cmap(n))))
    for cmap in b_col_maps:
        in_specs.append(pl.BlockSpec((None, 1, tn), lambda n, i, cmap=cmap: (layer, 0, cmap(n))))
    return pl.pallas_call(
        functools.partial(_wide_kernel, n_w=len(ws), n_b=len(bs), epilogue=epilogue,
                          w_is_transposed=w_is_transposed),
        grid=(n_blocks, m // tm),
        in_specs=in_specs,
        out_specs=pl.BlockSpec((tm, tn), lambda n, i: (i, n)),
        out_shape=jax.ShapeDtypeStruct((m, n_blocks * tn), out_dtype),
        scratch_shapes=[pltpu.VMEM((tn, k) if w_is_transposed else (k, tn), BF16) for _ in ws],
        compiler_params=_cparams(("arbitrary", "arbitrary")),
        name=name,
    )(u, *ws, *bs)


def _identity_epilogue(y):
    return y


def _swiglu_epilogue(g, up):
    return _silu(g) * up


def _glu_epilogue(a, g, ba, bg):
    return (a + ba) * _sigmoid(g + bg)


def _small_proj_kernel(u_ref, w_ref, o_ref):
    n, k = w_ref.shape
    w = jnp.concatenate([w_ref[...], jnp.zeros((LANES - n, k), F32)], axis=0)
    o_ref[...] = _dot_nt(u_ref[...], w.astype(BF16))


def small_proj(u, w_t_stack, layer, first_row, n, tm=1024):
    m, k = u.shape
    tm = min(tm, m)
    return pl.pallas_call(
        _small_proj_kernel,
        grid=(m // tm,),
        in_specs=[pl.BlockSpec((tm, k), lambda i: (i, 0)),
                  pl.BlockSpec((None, n, k), lambda i: (layer, first_row // n, 0))],
        out_specs=pl.BlockSpec((tm, LANES), lambda i: (i, 0)),
        out_shape=jax.ShapeDtypeStruct((m, LANES), F32),
        compiler_params=_cparams(("arbitrary",)),
        name="dt_proj",
    )(u, w_t_stack)


def _residual_epilogue(m, h, gpost, gnext, hout_ref, unext_ref):
    hn = h + _rms_rows(m, gpost)
    hout_ref[...] = hn
    if unext_ref is not None:
        unext_ref[...] = _rms_rows(hn, gnext).astype(unext_ref.dtype)


def _load_weight_bf16(w_hbm, layer, w16_ref, stage_ref, sem_ref):
    rows = stage_ref.shape[1]
    n_chunks = w16_ref.shape[0] // rows

    def chunk_copy(c):
        return pltpu.make_async_copy(w_hbm.at[layer, pl.ds(c * rows, rows), :],
                                     stage_ref.at[c % 2], sem_ref.at[c % 2])

    chunk_copy(0).start()
    for c in range(n_chunks):
        if c + 1 < n_chunks:
            chunk_copy(c + 1).start()
        chunk_copy(c).wait()
        w16_ref[c * rows:(c + 1) * rows, :] = stage_ref[c % 2].astype(BF16)


def _out_proj_kernel(*refs, n_a, with_next, layer):
    a_refs = refs[:n_a]
    w_hbm, h_ref, gpost_ref, gnext_ref, hout_ref = refs[n_a:n_a + 5]
    unext_ref = refs[n_a + 5] if with_next else None
    w16_ref, stage_ref, sem_ref = refs[-3:]

    @pl.when(pl.program_id(0) == 0)
    def _():
        _load_weight_bf16(w_hbm, layer, w16_ref, stage_ref, sem_ref)

    off = 0
    m = None
    for a_ref in a_refs:
        ka = a_ref.shape[1]
        part = _dot(a_ref[...], w16_ref[off:off + ka, :])
        m = part if m is None else m + part
        off += ka
    _residual_epilogue(m, h_ref[...], gpost_ref[...], gnext_ref[...], hout_ref, unext_ref)


def out_proj(acts, w_stack, layer, h, gpost, gnext, with_next, name, tm=256,
             stage_rows=WEIGHT_STAGE_ROWS):
    m = h.shape[0]
    _, kk, d = w_stack.shape
    tm = min(tm, m)
    stage_rows = min(stage_rows, kk)
    row = pl.BlockSpec((tm, d), lambda i: (i, 0))
    vec = pl.BlockSpec((1, d), lambda i: (0, 0))
    out_shape = [jax.ShapeDtypeStruct((m, d), F32)]
    out_specs = [row]
    if with_next:
        out_shape.append(jax.ShapeDtypeStruct((m, d), BF16))
        out_specs.append(row)
    res = pl.pallas_call(
        functools.partial(_out_proj_kernel, n_a=len(acts), with_next=with_next, layer=layer),
        grid=(m // tm,),
        in_specs=[pl.BlockSpec((tm, a.shape[1]), lambda i: (i, 0)) for a in acts] + [
            pl.BlockSpec(memory_space=pl.ANY), row, vec, vec],
        out_specs=out_specs,
        out_shape=out_shape,
        scratch_shapes=[pltpu.VMEM((kk, d), BF16), pltpu.VMEM((2, stage_rows, d), F32),
                        pltpu.SemaphoreType.DMA((2,))],
        compiler_params=_cparams(("arbitrary",)),
        name=name,
    )(*acts, w_stack, h, gpost.reshape(1, d), gnext.reshape(1, d))
    return (res[0], res[1]) if with_next else (res[0], None)


def _hgrn_kernel(q_ref, f_ref, v_ref, gate_ref, lb_ref, an_ref, o_ref, st_ref,
                 *, n_chunks, n_heads, part_heads):
    C, SUB, HD = HGRN_CHUNK, HGRN_SUB, HGRN_HEAD
    n_sub = C // SUB
    mid = SUB // 2 - 1
    width = part_heads * HD
    head_lanes = [slice(h * HD, (h + 1) * HD) for h in range(part_heads)]
    parts = [slice(p * width, (p + 1) * width) for p in range(n_heads // part_heads)]

    @pl.when(pl.program_id(2) == 0)
    def _():
        st_ref[...] = jnp.zeros_like(st_ref)

    tri = _lower_tri(C, BF16)
    causal = (lax.broadcasted_iota(jnp.int32, (C, C), 0)
              >= lax.broadcasted_iota(jnp.int32, (C, C), 1))
    def gates(sl, part):
        lb = lb_ref[:, part]
        f = (0.5 + 0.5 * lb) + (0.5 - 0.5 * lb) * jnp.tanh(0.5 * f_ref[sl, part])
        k = 1.0 - f
        log2_f = jnp.log(jnp.maximum(f, HGRN_F_MIN)) * LOG2_E
        qs = _silu(q_ref[sl, part].astype(F32))
        b = _dot_exact_rhs(tri, log2_f)
        return qs, k, b, v_ref[sl, part]

    def scores_and_state(vals, h0):
        qs, k, b, v16 = vals
        b_last = b[C - 1:C, :]

        q_in16 = (qs * jnp.exp2(b)).astype(BF16)
        k_end16 = (k * jnp.exp2(b_last - b)).astype(BF16)
        dec_last = jnp.exp2(b_last)
        sts = [st_ref[h0 + h] for h in range(part_heads)]
        o_heads = [_dot(q_in16[:, s], st.astype(BF16)) for s, st in zip(head_lanes, sts)]
        for h, (s, st) in enumerate(zip(head_lanes, sts)):
            row_decay = jnp.broadcast_to(dec_last[:, s], (HD, HD)).T
            st_ref[h0 + h] = st * row_decay + _dot_tn(k_end16[:, s], v16[:, s])

        blocks = [[] for _ in range(part_heads)]
        keys = []
        prev_ref = None
        for i in range(n_sub):
            rows = slice(i * SUB, (i + 1) * SUB)
            ref_b = b[i * SUB + mid:i * SUB + mid + 1, :]
            if keys:
                step = jnp.exp2(ref_b - prev_ref)
                keys = [kj * step for kj in keys]
            keys.append(k[rows, :] * jnp.exp2(ref_b - b[rows, :]))
            prev_ref = ref_b
            qt16 = (qs[rows, :] * jnp.exp2(b[rows, :] - ref_b)).astype(BF16)
            pad = [jnp.zeros(((n_sub - 1 - i) * SUB, width), F32)] if i < n_sub - 1 else []
            kt16 = jnp.concatenate(keys + pad, axis=0).astype(BF16)
            for h, s in enumerate(head_lanes):
                blocks[h].append(_dot_nt(qt16[:, s], kt16[:, s]))
        return o_heads, blocks

    def outputs(sl, part, vals, o_heads, blocks):
        v16 = vals[3]
        for h, s in enumerate(head_lanes):
            scores = jnp.where(causal, jnp.concatenate(blocks[h], axis=0), 0.0)
            o_heads[h] = o_heads[h] + _dot(scores.astype(BF16), v16[:, s])
        y = jnp.concatenate(
            [o * lax.rsqrt(jnp.mean(o * o, axis=-1, keepdims=True) + RMS_EPS) for o in o_heads],
            axis=1)
        y = y * an_ref[:, part] * _silu(gate_ref[sl, part].astype(F32))
        o_ref[sl, part] = y.astype(o_ref.dtype)

    def chunk(c, carry):
        sl = pl.ds(pl.multiple_of(c * C, C), C)
        vals = [gates(sl, part) for part in parts]
        mids = {}
        for p in range(len(parts) + 1):
            if p < len(parts):
                mids[p] = scores_and_state(vals[p], p * part_heads)
            if p >= 1:
                outputs(sl, parts[p - 1], vals[p - 1], *mids[p - 1])
        return carry

    lax.fori_loop(0, n_chunks, chunk, 0)


def hgrn_mix(pb, f_pre, lb, a_norm, batch, seq, d_model, rows_per_step=512):
    n_heads = d_model // HGRN_HEAD
    part_heads = min(HGRN_HEADS_PER_PART, n_heads)
    hb = min(HGRN_PARTS_PER_STEP * part_heads, n_heads)
    gw = hb * HGRN_HEAD
    n_groups = n_heads // hb
    cs = min(rows_per_step, seq)
    steps = seq // cs
    m = batch * seq

    def col(seg):
        return pl.BlockSpec((cs, gw), lambda b, h, s, seg=seg: (b * steps + s, seg * n_groups + h))

    vec = pl.BlockSpec((1, gw), lambda b, h, s: (0, h))
    return pl.pallas_call(
        functools.partial(_hgrn_kernel, n_chunks=cs // HGRN_CHUNK, n_heads=hb,
                          part_heads=part_heads),
        grid=(batch, n_groups, steps),
        in_specs=[col(0), col(0), col(1), col(2), vec, vec],
        out_specs=col(0),
        out_shape=jax.ShapeDtypeStruct((m, d_model), BF16),
        scratch_shapes=[pltpu.VMEM((hb, HGRN_HEAD, HGRN_HEAD), F32)],
        compiler_params=_cparams(("arbitrary", "arbitrary", "arbitrary")),
        name="hgrn2",
    )(pb, f_pre, pb, pb, lb.reshape(1, d_model), a_norm.reshape(1, d_model))


def _causal_conv(raw_ref, carry_ref, ext_ref, w_ref, b_ref, n_rows, n_taps, halo):
    first = halo - n_taps + 1
    outs = []
    for j in range(raw_ref.shape[1] // LANES):
        ln = slice(j * LANES, (j + 1) * LANES)
        raw = raw_ref[:, ln].astype(F32)
        ext_ref[j, 0:halo, :] = carry_ref[j]
        ext_ref[j, halo:halo + n_rows, :] = raw
        carry_ref[j] = raw[n_rows - halo:n_rows, :]
        acc = b_ref[:, ln] + w_ref[0:1, ln] * ext_ref[j, pl.ds(first, n_rows), :]
        for k in range(1, n_taps):
            acc = acc + w_ref[k:k + 1, ln] * ext_ref[j, pl.ds(first + k, n_rows), :]
        outs.append(acc)
    return jnp.concatenate(outs, axis=1)


def _ssd_kernel(x_ref, bm_ref, cm_ref, z_ref, dt_ref,
                cwx_ref, cwb_ref, cwc_ref, cbx_ref, cbb_ref, cbc_ref,
                dtb_ref, alog_ref, selp_ref, selw_ref, dskip_ref, norm_ref, o_ref,
                st_ref, carx_ref, carb_ref, carc_ref, extx_ref, extb_ref, extc_ref,
                *, heads_per_group, n_groups, n_heads):
    L, P, R, N = SSD_CHUNK, SSD_HEAD_DIM, heads_per_group, SSD_STATE
    gw = R * P
    groups = range(n_groups)
    g_lanes = [slice(j * gw, (j + 1) * gw) for j in groups]
    n_lanes = [slice(j * N, (j + 1) * N) for j in groups]

    @pl.when(pl.program_id(2) == 0)
    def _():
        st_ref[...] = jnp.zeros_like(st_ref)
        carx_ref[...] = jnp.zeros_like(carx_ref)
        carb_ref[...] = jnp.zeros_like(carb_ref)
        carc_ref[...] = jnp.zeros_like(carc_ref)

    halo = carx_ref.shape[1]
    xc = _silu(_causal_conv(x_ref, carx_ref, extx_ref, cwx_ref, cbx_ref, L, SSD_CONV, halo))
    bc16 = _silu(_causal_conv(bm_ref, carb_ref, extb_ref, cwb_ref, cbb_ref, L, SSD_CONV,
                              halo)).astype(BF16)
    cc16 = _silu(_causal_conv(cm_ref, carc_ref, extc_ref, cwc_ref, cbc_ref, L, SSD_CONV,
                              halo)).astype(BF16)

    lane = lax.broadcasted_iota(jnp.int32, (L, LANES), 1)
    dt = jnp.where(lane < n_heads, _softplus(dt_ref[...] + dtb_ref[...]), 0.0)
    dta = dt * (-LOG2_E * jnp.exp(alog_ref[...]))
    cs = _dot_exact_rhs(_lower_tri(L, BF16), dta)

    dt3 = _lane_stack3(dt, n_heads)
    cs3 = _lane_stack3(cs, n_heads)
    dt_x = [_dot(dt3, selp_ref[j]) for j in groups]
    cs_x = [_dot(cs3, selp_ref[j]) for j in groups]
    cs_w = [_dot(cs3, selw_ref[j]) for j in groups]

    rr = lax.broadcasted_iota(jnp.int32, (L, L), 0)
    cl = lax.broadcasted_iota(jnp.int32, (L, L), 1)
    causal = rr >= cl
    ys, xdt16s, cbs = [], [], []
    for j in groups:
        xdt = xc[:, g_lanes[j]] * dt_x[j]
        cs_last = cs_x[j][L - 1:L, :]
        st = st_ref[j]
        ys.append(_dot(cc16[:, n_lanes[j]], st.astype(BF16)) * jnp.exp2(cs_x[j]))
        st_ref[j] = st * jnp.exp2(cs_last) + _dot_tn(
            bc16[:, n_lanes[j]], (xdt * jnp.exp2(cs_last - cs_x[j])).astype(BF16))
        cb = _dot_nt(cc16[:, n_lanes[j]], bc16[:, n_lanes[j]])
        cbs.append(jnp.where(causal, cb, 0.0))
        xdt16s.append(xdt.astype(BF16))

    parts = [[] for _ in groups]
    for r in range(R):
        for j in groups:
            col = cs_w[j][:, r * L:(r + 1) * L]
            seg = col - col.T
            decay = jnp.exp2(jnp.minimum(seg, 0.0))
            parts[j].append(_dot((cbs[j] * decay).astype(BF16),
                                 xdt16s[j][:, r * P:(r + 1) * P]))

    for j in groups:
        ln = g_lanes[j]
        y = ys[j] + jnp.concatenate(parts[j], axis=1) + dskip_ref[:, ln] * xc[:, ln]
        yz = y * _silu(z_ref[:, ln].astype(F32))
        o_ref[:, ln] = _rms_rows(yz, norm_ref[:, ln]).astype(o_ref.dtype)


def ssd_mix(pb, dt_raw, conv_w, conv_b, dt_bias, a_log, d_skip, b_norm,
            batch, seq, d_model):
    L, G, N = SSD_CHUNK, SSD_GROUPS, SSD_STATE
    n_heads = d_model // SSD_HEAD_DIM
    R = n_heads // G
    gw = d_model // G
    steps = seq // L
    m = batch * seq
    halo = 8
    gs = SSD_GROUPS_PER_STEP
    gsw, gsn = gs * gw, gs * N
    bn0 = 5 * d_model // gsn
    cwb0 = d_model // gsn

    def rep_heads(v):
        return jnp.repeat(v.astype(F32), SSD_HEAD_DIM).reshape(1, d_model)

    def pad_heads(v):
        return jnp.pad(v.astype(F32), (0, LANES - n_heads)).reshape(1, LANES)

    def select(width):
        row = jnp.arange(LANES, dtype=jnp.int32)[None, :, None]
        lane = jnp.arange(R * width, dtype=jnp.int32)[None, None, :]
        grp = jnp.arange(G, dtype=jnp.int32)[:, None, None]
        hit = (row % n_heads == grp * R + lane // width) & (row < 3 * n_heads)
        return hit.astype(BF16)

    row_g = lambda off: pl.BlockSpec((L, gsw), lambda b, g, s, off=off: (b * steps + s, off + g))
    row_n = lambda off: pl.BlockSpec((L, gsn), lambda b, g, s, off=off: (b * steps + s, off + g))
    par_g = lambda rows: pl.BlockSpec((rows, gsw), lambda b, g, s: (0, g))
    par_n = lambda rows, off: pl.BlockSpec((rows, gsn), lambda b, g, s, off=off: (0, off + g))
    head_vec = pl.BlockSpec((1, LANES), lambda b, g, s: (0, 0))
    sel = lambda width: pl.BlockSpec((gs, LANES, R * width), lambda b, g, s: (g, 0, 0))
    cb2 = conv_b.reshape(1, -1)
    n_steps_g = G // gs

    return pl.pallas_call(
        functools.partial(_ssd_kernel, heads_per_group=R, n_groups=gs, n_heads=n_heads),
        grid=(batch, n_steps_g, steps),
        in_specs=[row_g(4 * n_steps_g), row_n(bn0), row_n(bn0 + n_steps_g), row_g(3 * n_steps_g),
                  pl.BlockSpec((L, LANES), lambda b, g, s: (b * steps + s, 0)),
                  par_g(SSD_CONV), par_n(SSD_CONV, cwb0), par_n(SSD_CONV, cwb0 + n_steps_g),
                  par_g(1), par_n(1, cwb0), par_n(1, cwb0 + n_steps_g),
                  head_vec, head_vec, sel(SSD_HEAD_DIM), sel(L), par_g(1), par_g(1)],
        out_specs=pl.BlockSpec((L, gsw), lambda b, g, s: (b * steps + s, g)),
        out_shape=jax.ShapeDtypeStruct((m, d_model), BF16),
        scratch_shapes=[pltpu.VMEM((gs, N, gw), F32),
                        pltpu.VMEM((gsw // LANES, halo, LANES), F32),
                        pltpu.VMEM((gsn // LANES, halo, LANES), F32),
                        pltpu.VMEM((gsn // LANES, halo, LANES), F32),
                        pltpu.VMEM((gsw // LANES, halo + L, LANES), F32),
                        pltpu.VMEM((gsn // LANES, halo + L, LANES), F32),
                        pltpu.VMEM((gsn // LANES, halo + L, LANES), F32)],
        compiler_params=_cparams(("arbitrary", "arbitrary", "arbitrary")),
        name="ssd",
    )(pb, pb, pb, pb, dt_raw,
      conv_w, conv_w, conv_w, cb2, cb2, cb2,
      pad_heads(dt_bias), pad_heads(a_log),
      select(SSD_HEAD_DIM), select(L), rep_heads(d_skip), b_norm.reshape(1, d_model))


def _conf_tail_kernel(c_ref, dww_ref, dwb_ref, lng_ref, lnb_ref, w2_hbm, b2_ref,
                      h_ref, gpost_ref, gnext_ref, hout_ref, unext_ref,
                      carry_ref, ext_ref, conv_ref, w2_ref, stage_ref, sem_ref,
                      *, row_block, layer):
    tm, d = c_ref.shape
    halo = CONF_HALO
    first = halo - (CONF_KERNEL - 1)
    n_tiles = d // LANES

    @pl.when((pl.program_id(0) == 0) & (pl.program_id(1) == 0))
    def _():
        _load_weight_bf16(w2_hbm, layer, w2_ref, stage_ref, sem_ref)

    @pl.when(pl.program_id(1) == 0)
    def _():
        carry_ref[...] = jnp.zeros_like(carry_ref)

    for j in range(n_tiles):
        ln = slice(j * LANES, (j + 1) * LANES)
        ext_ref[j, 0:halo, :] = carry_ref[j]
        ext_ref[j, halo:halo + tm, :] = c_ref[:, ln]
        carry_ref[j] = c_ref[tm - halo:tm, ln]

    def col_tile(j, carry):
        cj = pl.ds(pl.multiple_of(j * LANES, LANES), LANES)
        for rb in range(tm // row_block):
            base = rb * row_block
            acc = dwb_ref[:, cj] + dww_ref[0:1, cj] * ext_ref[j, pl.ds(base + first, row_block), :]
            for k in range(1, CONF_KERNEL):
                acc = acc + dww_ref[k:k + 1, cj] * ext_ref[j, pl.ds(base + first + k, row_block), :]
            conv_ref[base:base + row_block, cj] = acc
        return carry

    lax.fori_loop(0, n_tiles, col_tile, 0)

    x = conv_ref[...]
    mu = jnp.mean(x, axis=-1, keepdims=True)
    xc = x - mu
    var = jnp.mean(xc * xc, axis=-1, keepdims=True)
    y = _silu(xc * lax.rsqrt(var + LN_EPS) * lng_ref[...] + lnb_ref[...])
    m = _dot(y.astype(BF16), w2_ref[...]) + b2_ref[...]
    _residual_epilogue(m, h_ref[...], gpost_ref[...], gnext_ref[...], hout_ref, unext_ref)


def conf_tail(c, dw_w, dw_b, ln_g, ln_b, w2_stack, layer, b2, h, gpost, gnext,
              batch, seq, tm=256, row_block=64, stage_rows=WEIGHT_STAGE_ROWS):
    m, d = c.shape
    tm = min(tm, seq)
    row_block = min(row_block, tm)
    stage_rows = min(stage_rows, d)
    steps = seq // tm
    row = pl.BlockSpec((tm, d), lambda b, s: (b * steps + s, 0))
    vec = pl.BlockSpec((1, d), lambda b, s: (0, 0))
    dww = jnp.pad(dw_w, ((0, CONF_HALO - CONF_KERNEL), (0, 0)))
    return pl.pallas_call(
        functools.partial(_conf_tail_kernel, row_block=row_block, layer=layer),
        grid=(batch, steps),
        in_specs=[row, pl.BlockSpec((CONF_HALO, d), lambda b, s: (0, 0)), vec, vec, vec,
                  pl.BlockSpec(memory_space=pl.ANY),
                  vec, row, vec, vec],
        out_specs=[row, row],
        out_shape=[jax.ShapeDtypeStruct((m, d), F32), jax.ShapeDtypeStruct((m, d), BF16)],
        scratch_shapes=[pltpu.VMEM((d // LANES, CONF_HALO, LANES), F32),
                        pltpu.VMEM((d // LANES, CONF_HALO + tm, LANES), F32),
                        pltpu.VMEM((tm, d), F32),
                        pltpu.VMEM((d, d), BF16), pltpu.VMEM((2, stage_rows, d), F32),
                        pltpu.SemaphoreType.DMA((2,))],
        compiler_params=_cparams(("arbitrary", "arbitrary")),
        name="conf_tail",
    )(c, dww, dw_b.reshape(1, d), ln_g.reshape(1, d), ln_b.reshape(1, d), w2_stack,
      b2.reshape(1, d), h, gpost.reshape(1, d), gnext.reshape(1, d))


def kernel(x, mix_pre_g, mix_post_g, ffn_pre_g, ffn_post_g, hgrn_lb_logits, even_w_in,
           hgrn_norm_g, ssd_conv_w, ssd_conv_b, ssd_dt_bias, ssd_a_log, ssd_d, ssd_norm_g,
           even_w_out, conf_w1, conf_b1, conf_dw_w, conf_dw_b, conf_ln_g, conf_ln_b,
           conf_w2, conf_b2, ffn_w_gate, ffn_w_up, ffn_w_down):
    batch, seq, d = x.shape
    depth = mix_pre_g.shape[0]
    hidden = ffn_w_gate.shape[2]
    m = batch * seq
    tn = 512
    tn_in = min(1024, d)
    main_cols = 6 * d + 2 * SSD_GROUPS * SSD_STATE
    n_ssd_heads = d // SSD_HEAD_DIM
    f_blocks = d // tn_in

    lb_p = jax.nn.softmax(hgrn_lb_logits.astype(F32), axis=0)
    lower_bounds = jnp.cumsum(lb_p, axis=0) - lb_p[0]

    w_in_t = jnp.swapaxes(even_w_in, 1, 2)
    conf_b1_3d = conf_b1.reshape(conf_b1.shape[0], 1, -1)
    plus = lambda off: (lambda n: n + off)

    h = x.reshape(m, d)
    u = prenorm(h, mix_pre_g[0])
    for layer in range(depth):
        i = layer // 2
        if layer % 2 == 0:
            skip_f = lambda n: n + jnp.where(n >= f_blocks, f_blocks, 0)
            pb = wide_proj(u, [w_in_t], i, [skip_f], [], [], (main_cols - d) // tn_in,
                           _identity_epilogue, BF16, "even_in_proj", tn=tn_in,
                           w_is_transposed=True)
            f_pre = wide_proj(u, [w_in_t], i, [plus(f_blocks)], [], [], f_blocks,
                              _identity_epilogue, F32, "even_f_proj", tn=tn_in,
                              w_is_transposed=True)
            dt_raw = small_proj(u, w_in_t, i, main_cols, n_ssd_heads)
            o_a = hgrn_mix(pb, f_pre, lower_bounds[i], hgrn_norm_g[i], batch, seq, d)
            o_b = ssd_mix(pb, dt_raw, ssd_conv_w[i], ssd_conv_b[i], ssd_dt_bias[i],
                          ssd_a_log[i], ssd_d[i], ssd_norm_g[i], batch, seq, d)
            h, u = out_proj([o_a, o_b], even_w_out, i, h, mix_post_g[layer],
                            ffn_pre_g[layer], True, "even_out_proj")
        else:
            c = wide_proj(u, [conf_w1, conf_w1], i, [plus(0), plus(d // tn)],
                          [conf_b1_3d, conf_b1_3d], [plus(0), plus(d // tn)], d // tn,
                          _glu_epilogue, F32, "conf_glu", tn=tn)
            h, u = conf_tail(c, conf_dw_w[i], conf_dw_b[i], conf_ln_g[i], conf_ln_b[i],
                             conf_w2, i, conf_b2[i], h, mix_post_g[layer],
                             ffn_pre_g[layer], batch, seq)
        act = wide_proj(u, [ffn_w_gate, ffn_w_up], layer, [plus(0), plus(0)], [], [],
                        hidden // tn, _swiglu_epilogue, BF16, "ffn_in", tn=tn)
        last = layer == depth - 1
        gnext = mix_pre_g[layer] if last else mix_pre_g[layer + 1]
        h, u = out_proj([act], ffn_w_down, layer, h, ffn_post_g[layer], gnext, not last,
                        "ffn_out")
    return h.reshape(batch, seq, d)
```

```python
import functools

import jax
import jax.numpy as jnp
from jax import lax
from jax.experimental import pallas as pl
from jax.experimental.pallas import tpu as pltpu

F32 = jnp.float32
BF16 = jnp.bfloat16

LOG2_E = 1.4426950408889634
RMS_EPS = 1e-6
LN_EPS = 1e-5
HGRN_F_MIN = 1e-6
HGRN_HEAD = 128
HGRN_CHUNK = 64
HGRN_SUB = 8
HGRN_HEADS_PER_PART = 4
HGRN_SEQS_PER_STEP = 2
HGRN_PARTS_PER_STEP = 4
SSD_HEAD_DIM = 64
SSD_GROUPS = 4
SSD_STATE = 128
SSD_CONV = 4
SSD_CHUNK = 128
SSD_GROUPS_PER_STEP = 4
CONF_KERNEL = 31
CONF_HALO = 32
WEIGHT_STAGE_ROWS = 512
LANES = 128
VMEM_LIMIT = 56 * 1024 * 1024


def _cparams(semantics):
    return pltpu.CompilerParams(dimension_semantics=semantics,
                                vmem_limit_bytes=VMEM_LIMIT)


def _sigmoid(x):
    return 0.5 * jnp.tanh(0.5 * x) + 0.5


def _silu(x):
    t = 0.5 * x
    return t * jnp.tanh(t) + t


def _softplus(x):
    return jnp.maximum(x, 0.0) + jnp.log1p(jnp.exp(-jnp.abs(x)))


def _rms_rows(x, g, eps=RMS_EPS):
    ms = jnp.mean(x * x, axis=-1, keepdims=True)
    return x * lax.rsqrt(ms + eps) * g


def _dot(a, b):
    return jnp.dot(a, b, preferred_element_type=F32)


def _dot_nt(a, b):
    return lax.dot_general(a, b, (((1,), (1,)), ((), ())), preferred_element_type=F32)


def _dot_tn(a, b):
    return lax.dot_general(a, b, (((0,), (0,)), ((), ())), preferred_element_type=F32)


def _split3(x):
    hi = x.astype(BF16)
    r1 = x - hi.astype(F32)
    mid = r1.astype(BF16)
    lo = (r1 - mid.astype(F32)).astype(BF16)
    return hi, mid, lo


def _dot_exact_rhs(sel, x):
    return _dot(jnp.concatenate([sel, sel, sel], axis=1),
                jnp.concatenate(_split3(x), axis=0))


def _lane_stack3(x, n):
    hi, mid, lo = _split3(x)
    return (hi.astype(F32) + pltpu.roll(mid.astype(F32), n, 1)
            + pltpu.roll(lo.astype(F32), 2 * n, 1)).astype(BF16)


def _lower_tri(n, dtype):
    r = lax.broadcasted_iota(jnp.int32, (n, n), 0)
    c = lax.broadcasted_iota(jnp.int32, (n, n), 1)
    return (r >= c).astype(dtype)


def _prenorm_kernel(x_ref, g_ref, o_ref):
    o_ref[...] = _rms_rows(x_ref[...], g_ref[...]).astype(o_ref.dtype)


def prenorm(x, g, tm=512):
    m, d = x.shape
    tm = min(tm, m)
    return pl.pallas_call(
        _prenorm_kernel,
        grid=(m // tm,),
        in_specs=[pl.BlockSpec((tm, d), lambda i: (i, 0)),
                  pl.BlockSpec((1, d), lambda i: (0, 0))],
        out_specs=pl.BlockSpec((tm, d), lambda i: (i, 0)),
        out_shape=jax.ShapeDtypeStruct((m, d), BF16),
        compiler_params=_cparams(("arbitrary",)),
        name="prenorm",
    )(x, g.reshape(1, d))


def _wide_kernel(*refs, n_w, n_b, epilogue, w_is_transposed):
    u_ref = refs[0]
    w_refs = refs[1:1 + n_w]
    b_refs = refs[1 + n_w:1 + n_w + n_b]
    o_ref = refs[1 + n_w + n_b]
    s_refs = refs[2 + n_w + n_b:]

    @pl.when(pl.program_id(1) == 0)
    def _():
        for w, s in zip(w_refs, s_refs):
            s[...] = w[...].astype(BF16)

    u = u_ref[...]
    dot = _dot_nt if w_is_transposed else _dot
    ys = [dot(u, s[...]) for s in s_refs]
    o_ref[...] = epilogue(*ys, *[b[...] for b in b_refs]).astype(o_ref.dtype)


def wide_proj(u, ws, layer, w_col_maps, bs, b_col_maps, n_blocks, epilogue, out_dtype,
              name, tn=512, tm=1024, w_is_transposed=False):
    m, k = u.shape
    tm = min(tm, m)
    in_specs = [pl.BlockSpec((tm, k), lambda n, i: (i, 0))]
    for cmap in w_col_maps:
        if w_is_transposed:
            in_specs.append(pl.BlockSpec((None, tn, k), lambda n, i, cmap=cmap: (layer, cmap(n), 0)))
        else:
            in_specs.append(pl.BlockSpec((None, k, tn), lambda n, i, cmap=cmap: (layer, 0, cmap(n))))
    for cmap in b_col_maps:
        in_specs.append(pl.BlockSpec((None, 1, tn), lambda n, i, cmap=cmap: (layer, 0, cmap(n))))
    return pl.pallas_call(
        functools.partial(_wide_kernel, n_w=len(ws), n_b=len(bs), epilogue=epilogue,
                          w_is_transposed=w_is_transposed),
        grid=(n_blocks, m // tm),
        in_specs=in_specs,
        out_specs=pl.BlockSpec((tm, tn), lambda n, i: (i, n)),
        out_shape=jax.ShapeDtypeStruct((m, n_blocks * tn), out_dtype),
        scratch_shapes=[pltpu.VMEM((tn, k) if w_is_transposed else (k, tn), BF16) for _ in ws],
        compiler_params=_cparams(("arbitrary", "arbitrary")),
        name=name,
    )(u, *ws, *bs)


def _identity_epilogue(y):
    return y


def _swiglu_epilogue(g, up):
    return _silu(g) * up


def _glu_epilogue(a, g, ba, bg):
    return (a + ba) * _sigmoid(g + bg)


def _small_proj_kernel(u_ref, w_ref, o_ref):
    n, k = w_ref.shape
    w = jnp.concatenate([w_ref[...], jnp.zeros((LANES - n, k), F32)], axis=0)
    o_ref[...] = _dot_nt(u_ref[...], w.astype(BF16))


def small_proj(u, w_t_stack, layer, first_row, n, tm=1024):
    m, k = u.shape
    tm = min(tm, m)
    return pl.pallas_call(
        _small_proj_kernel,
        grid=(m // tm,),
        in_specs=[pl.BlockSpec((tm, k), lambda i: (i, 0)),
                  pl.BlockSpec((None, n, k), lambda i: (layer, first_row // n, 0))],
        out_specs=pl.BlockSpec((tm, LANES), lambda i: (i, 0)),
        out_shape=jax.ShapeDtypeStruct((m, LANES), F32),
        compiler_params=_cparams(("arbitrary",)),
        name="dt_proj",
    )(u, w_t_stack)


def _residual_epilogue(m, h, gpost, gnext, hout_ref, unext_ref):
    hn = h + _rms_rows(m, gpost)
    hout_ref[...] = hn
    if unext_ref is not None:
        unext_ref[...] = _rms_rows(hn, gnext).astype(unext_ref.dtype)


def _load_weight_bf16(w_hbm, layer, w16_ref, stage_ref, sem_ref):
    rows = stage_ref.shape[1]
    n_chunks = w16_ref.shape[0] // rows

    def chunk_copy(c):
        return pltpu.make_async_copy(w_hbm.at[layer, pl.ds(c * rows, rows), :],
                                     stage_ref.at[c % 2], sem_ref.at[c % 2])

    chunk_copy(0).start()
    for c in range(n_chunks):
        if c + 1 < n_chunks:
            chunk_copy(c + 1).start()
        chunk_copy(c).wait()
        w16_ref[c * rows:(c + 1) * rows, :] = stage_ref[c % 2].astype(BF16)


def _out_proj_kernel(*refs, n_a, with_next, layer):
    a_refs = refs[:n_a]
    w_hbm, h_ref, gpost_ref, gnext_ref, hout_ref = refs[n_a:n_a + 5]
    unext_ref = refs[n_a + 5] if with_next else None
    w16_ref, stage_ref, sem_ref = refs[-3:]

    @pl.when(pl.program_id(0) == 0)
    def _():
        _load_weight_bf16(w_hbm, layer, w16_ref, stage_ref, sem_ref)

    off = 0
    m = None
    for a_ref in a_refs:
        ka = a_ref.shape[1]
        part = _dot(a_ref[...], w16_ref[off:off + ka, :])
        m = part if m is None else m + part
        off += ka
    _residual_epilogue(m, h_ref[...], gpost_ref[...], gnext_ref[...], hout_ref, unext_ref)


def out_proj(acts, w_stack, layer, h, gpost, gnext, with_next, name, tm=256,
             stage_rows=WEIGHT_STAGE_ROWS):
    m = h.shape[0]
    _, kk, d = w_stack.shape
    tm = min(tm, m)
    stage_rows = min(stage_rows, kk)
    row = pl.BlockSpec((tm, d), lambda i: (i, 0))
    vec = pl.BlockSpec((1, d), lambda i: (0, 0))
    out_shape = [jax.ShapeDtypeStruct((m, d), F32)]
    out_specs = [row]
    if with_next:
        out_shape.append(jax.ShapeDtypeStruct((m, d), BF16))
        out_specs.append(row)
    res = pl.pallas_call(
        functools.partial(_out_proj_kernel, n_a=len(acts), with_next=with_next, layer=layer),
        grid=(m // tm,),
        in_specs=[pl.BlockSpec((tm, a.shape[1]), lambda i: (i, 0)) for a in acts] + [
            pl.BlockSpec(memory_space=pl.ANY), row, vec, vec],
        out_specs=out_specs,
        out_shape=out_shape,
        scratch_shapes=[pltpu.VMEM((kk, d), BF16), pltpu.VMEM((2, stage_rows, d), F32),
                        pltpu.SemaphoreType.DMA((2,))],
        compiler_params=_cparams(("arbitrary",)),
        name=name,
    )(*acts, w_stack, h, gpost.reshape(1, d), gnext.reshape(1, d))
    return (res[0], res[1]) if with_next else (res[0], None)


def _hgrn_kernel(q_ref, f_ref, v_ref, gate_ref, lb_ref, an_ref, o_ref, st_ref,
                 *, n_chunks, n_heads, part_heads, n_seq):
    C, SUB, HD = HGRN_CHUNK, HGRN_SUB, HGRN_HEAD
    n_sub = C // SUB
    mid = SUB // 2 - 1
    width = part_heads * HD
    head_lanes = [slice(h * HD, (h + 1) * HD) for h in range(part_heads)]
    parts = [slice(p * width, (p + 1) * width) for p in range(n_heads // part_heads)]

    @pl.when(pl.program_id(2) == 0)
    def _():
        st_ref[...] = jnp.zeros_like(st_ref)

    tri = _lower_tri(C, BF16)
    causal = (lax.broadcasted_iota(jnp.int32, (C, C), 0)
              >= lax.broadcasted_iota(jnp.int32, (C, C), 1))
    def gates(sl, bi, part):
        lb = lb_ref[:, part]
        f = (0.5 + 0.5 * lb) + (0.5 - 0.5 * lb) * jnp.tanh(0.5 * f_ref[bi, sl, part])
        k = 1.0 - f
        log2_f = jnp.log(jnp.maximum(f, HGRN_F_MIN)) * LOG2_E
        qs = _silu(q_ref[bi, sl, part].astype(F32))
        b = _dot_exact_rhs(tri, log2_f)
        return qs, k, b, v_ref[bi, sl, part]

    def scores_and_state(vals, h0):
        qs, k, b, v16 = vals
        b_last = b[C - 1:C, :]

        q_in16 = (qs * jnp.exp2(b)).astype(BF16)
        k_end16 = (k * jnp.exp2(b_last - b)).astype(BF16)
        dec_last = jnp.exp2(b_last)
        sts = [st_ref[h0 + h] for h in range(part_heads)]
        o_heads = [_dot(q_in16[:, s], st.astype(BF16)) for s, st in zip(head_lanes, sts)]
        for h, (s, st) in enumerate(zip(head_lanes, sts)):
            row_decay = jnp.broadcast_to(dec_last[:, s], (HD, HD)).T
            st_ref[h0 + h] = st * row_decay + _dot_tn(k_end16[:, s], v16[:, s])

        blocks = [[] for _ in range(part_heads)]
        keys = []
        prev_ref = None
        for i in range(n_sub):
            rows = slice(i * SUB, (i + 1) * SUB)
            ref_b = b[i * SUB + mid:i * SUB + mid + 1, :]
            if keys:
                step = jnp.exp2(ref_b - prev_ref)
                keys = [kj * step for kj in keys]
            keys.append(k[rows, :] * jnp.exp2(ref_b - b[rows, :]))
            prev_ref = ref_b
            qt16 = (qs[rows, :] * jnp.exp2(b[rows, :] - ref_b)).astype(BF16)
            pad = [jnp.zeros(((n_sub - 1 - i) * SUB, width), F32)] if i < n_sub - 1 else []
            kt16 = jnp.concatenate(keys + pad, axis=0).astype(BF16)
            for h, s in enumerate(head_lanes):
                blocks[h].append(_dot_nt(qt16[:, s], kt16[:, s]))
        return o_heads, blocks

    def outputs(sl, bi, part, vals, o_heads, blocks):
        v16 = vals[3]
        for h, s in enumerate(head_lanes):
            scores = jnp.where(causal, jnp.concatenate(blocks[h], axis=0), 0.0)
            o_heads[h] = o_heads[h] + _dot(scores.astype(BF16), v16[:, s])
        y = jnp.concatenate(
            [o * lax.rsqrt(jnp.mean(o * o, axis=-1, keepdims=True) + RMS_EPS) for o in o_heads],
            axis=1)
        y = y * an_ref[:, part] * _silu(gate_ref[bi, sl, part].astype(F32))
        o_ref[bi, sl, part] = y.astype(o_ref.dtype)

    def chunk(c, carry):
        sl = pl.ds(pl.multiple_of(c * C, C), C)
        units = [(bi, part) for bi in range(n_seq) for part in parts]
        vals = [gates(sl, bi, part) for bi, part in units]
        mids = {}
        for p in range(len(units) + 1):
            if p < len(units):
                mids[p] = scores_and_state(vals[p], p * part_heads)
            if p >= 1:
                outputs(sl, *units[p - 1], vals[p - 1], *mids[p - 1])
        return carry

    lax.fori_loop(0, n_chunks, chunk, 0)


def hgrn_mix(pb, f_pre, lb, a_norm, batch, seq, d_model, rows_per_step=256):
    n_heads = d_model // HGRN_HEAD
    part_heads = min(HGRN_HEADS_PER_PART, n_heads)
    hb = min(HGRN_PARTS_PER_STEP * part_heads, n_heads)
    gw = hb * HGRN_HEAD
    n_groups = n_heads // hb
    n_seq = min(HGRN_SEQS_PER_STEP, batch)
    cs = min(rows_per_step, seq)
    m = batch * seq

    def col(seg):
        return pl.BlockSpec((n_seq, cs, gw), lambda b, h, s, seg=seg: (b, s, seg * n_groups + h))

    vec = pl.BlockSpec((1, gw), lambda b, h, s: (0, h))
    pb3 = pb.reshape(batch, seq, -1)
    out = pl.pallas_call(
        functools.partial(_hgrn_kernel, n_chunks=cs // HGRN_CHUNK, n_heads=hb,
                          part_heads=part_heads, n_seq=n_seq),
        grid=(batch // n_seq, n_groups, seq // cs),
        in_specs=[col(0), col(0), col(1), col(2), vec, vec],
        out_specs=col(0),
        out_shape=jax.ShapeDtypeStruct((batch, seq, d_model), BF16),
        scratch_shapes=[pltpu.VMEM((n_seq * hb, HGRN_HEAD, HGRN_HEAD), F32)],
        compiler_params=_cparams(("arbitrary", "arbitrary", "arbitrary")),
        name="hgrn2",
    )(pb3, f_pre.reshape(batch, seq, d_model), pb3, pb3,
      lb.reshape(1, d_model), a_norm.reshape(1, d_model))
    return out.reshape(m, d_model)


def _causal_conv(raw_ref, carry_ref, ext_ref, w_ref, b_ref, n_rows, n_taps, halo):
    first = halo - n_taps + 1
    outs = []
    for j in range(raw_ref.shape[1] // LANES):
        ln = slice(j * LANES, (j + 1) * LANES)
        raw = raw_ref[:, ln].astype(F32)
        ext_ref[j, 0:halo, :] = carry_ref[j]
        ext_ref[j, halo:halo + n_rows, :] = raw
        carry_ref[j] = raw[n_rows - halo:n_rows, :]
        acc = b_ref[:, ln] + w_ref[0:1, ln] * ext_ref[j, pl.ds(first, n_rows), :]
        for k in range(1, n_taps):
            acc = acc + w_ref[k:k + 1, ln] * ext_ref[j, pl.ds(first + k, n_rows), :]
        outs.append(acc)
    return jnp.concatenate(outs, axis=1)


def _ssd_kernel(x_ref, bm_ref, cm_ref, z_ref, dt_ref,
                cwx_ref, cwb_ref, cwc_ref, cbx_ref, cbb_ref, cbc_ref,
                dtb_ref, alog_ref, selp_ref, selw_ref, dskip_ref, norm_ref, o_ref,
                st_ref, carx_ref, carb_ref, carc_ref, extx_ref, extb_ref, extc_ref,
                *, heads_per_group, n_groups, n_heads):
    L, P, R, N = SSD_CHUNK, SSD_HEAD_DIM, heads_per_group, SSD_STATE
    gw = R * P
    groups = range(n_groups)
    g_lanes = [slice(j * gw, (j + 1) * gw) for j in groups]
    n_lanes = [slice(j * N, (j + 1) * N) for j in groups]

    @pl.when(pl.program_id(2) == 0)
    def _():
        st_ref[...] = jnp.zeros_like(st_ref)
        carx_ref[...] = jnp.zeros_like(carx_ref)
        carb_ref[...] = jnp.zeros_like(carb_ref)
        carc_ref[...] = jnp.zeros_like(carc_ref)

    halo = carx_ref.shape[1]
    xc = _silu(_causal_conv(x_ref, carx_ref, extx_ref, cwx_ref, cbx_ref, L, SSD_CONV, halo))
    bc16 = _silu(_causal_conv(bm_ref, carb_ref, extb_ref, cwb_ref, cbb_ref, L, SSD_CONV,
                              halo)).astype(BF16)
    cc16 = _silu(_causal_conv(cm_ref, carc_ref, extc_ref, cwc_ref, cbc_ref, L, SSD_CONV,
                              halo)).astype(BF16)

    lane = lax.broadcasted_iota(jnp.int32, (L, LANES), 1)
    dt = jnp.where(lane < n_heads, _softplus(dt_ref[...] + dtb_ref[...]), 0.0)
    dta = dt * (-LOG2_E * jnp.exp(alog_ref[...]))
    cs = _dot_exact_rhs(_lower_tri(L, BF16), dta)

    dt3 = _lane_stack3(dt, n_heads)
    cs3 = _lane_stack3(cs, n_heads)
    dt_x = [_dot(dt3, selp_ref[j]) for j in groups]
    cs_x = [_dot(cs3, selp_ref[j]) for j in groups]
    cs_w = [_dot(cs3, selw_ref[j]) for j in groups]

    rr = lax.broadcasted_iota(jnp.int32, (L, L), 0)
    cl = lax.broadcasted_iota(jnp.int32, (L, L), 1)
    causal = rr >= cl
    ys, xdt16s, cbs = [], [], []
    for j in groups:
        xdt = xc[:, g_lanes[j]] * dt_x[j]
        cs_last = cs_x[j][L - 1:L, :]
        st = st_ref[j]
        ys.append(_dot(cc16[:, n_lanes[j]], st.astype(BF16)) * jnp.exp2(cs_x[j]))
        st_ref[j] = st * jnp.exp2(cs_last) + _dot_tn(
            bc16[:, n_lanes[j]], (xdt * jnp.exp2(cs_last - cs_x[j])).astype(BF16))
        cb = _dot_nt(cc16[:, n_lanes[j]], bc16[:, n_lanes[j]])
        cbs.append(jnp.where(causal, cb, 0.0))
        xdt16s.append(xdt.astype(BF16))

    parts = [[] for _ in groups]
    for r in range(R):
        for j in groups:
            col = cs_w[j][:, r * L:(r + 1) * L]
            seg = col - col.T
            decay = jnp.exp2(jnp.minimum(seg, 0.0))
            parts[j].append(_dot((cbs[j] * decay).astype(BF16),
                                 xdt16s[j][:, r * P:(r + 1) * P]))

    for j in groups:
        ln = g_lanes[j]
        y = ys[j] + jnp.concatenate(parts[j], axis=1) + dskip_ref[:, ln] * xc[:, ln]
        yz = y * _silu(z_ref[:, ln].astype(F32))
        o_ref[:, ln] = _rms_rows(yz, norm_ref[:, ln]).astype(o_ref.dtype)


def ssd_mix(pb, dt_raw, conv_w, conv_b, dt_bias, a_log, d_skip, b_norm,
            batch, seq, d_model):
    L, G, N = SSD_CHUNK, SSD_GROUPS, SSD_STATE
    n_heads = d_model // SSD_HEAD_DIM
    R = n_heads // G
    gw = d_model // G
    steps = seq // L
    m = batch * seq
    halo = 8
    gs = SSD_GROUPS_PER_STEP
    gsw, gsn = gs * gw, gs * N
    bn0 = 5 * d_model // gsn
    cwb0 = d_model // gsn

    def rep_heads(v):
        return jnp.repeat(v.astype(F32), SSD_HEAD_DIM).reshape(1, d_model)

    def pad_heads(v):
        return jnp.pad(v.astype(F32), (0, LANES - n_heads)).reshape(1, LANES)

    def select(width):
        row = jnp.arange(LANES, dtype=jnp.int32)[None, :, None]
        lane = jnp.arange(R * width, dtype=jnp.int32)[None, None, :]
        grp = jnp.arange(G, dtype=jnp.int32)[:, None, None]
        hit = (row % n_heads == grp * R + lane // width) & (row < 3 * n_heads)
        return hit.astype(BF16)

    row_g = lambda off: pl.BlockSpec((L, gsw), lambda b, g, s, off=off: (b * steps + s, off + g))
    row_n = lambda off: pl.BlockSpec((L, gsn), lambda b, g, s, off=off: (b * steps + s, off + g))
    par_g = lambda rows: pl.BlockSpec((rows, gsw), lambda b, g, s: (0, g))
    par_n = lambda rows, off: pl.BlockSpec((rows, gsn), lambda b, g, s, off=off: (0, off + g))
    head_vec = pl.BlockSpec((1, LANES), lambda b, g, s: (0, 0))
    sel = lambda width: pl.BlockSpec((gs, LANES, R * width), lambda b, g, s: (g, 0, 0))
    cb2 = conv_b.reshape(1, -1)
    n_steps_g = G // gs

    return pl.pallas_call(
        functools.partial(_ssd_kernel, heads_per_group=R, n_groups=gs, n_heads=n_heads),
        grid=(batch, n_steps_g, steps),
        in_specs=[row_g(4 * n_steps_g), row_n(bn0), row_n(bn0 + n_steps_g), row_g(3 * n_steps_g),
                  pl.BlockSpec((L, LANES), lambda b, g, s: (b * steps + s, 0)),
                  par_g(SSD_CONV), par_n(SSD_CONV, cwb0), par_n(SSD_CONV, cwb0 + n_steps_g),
                  par_g(1), par_n(1, cwb0), par_n(1, cwb0 + n_steps_g),
                  head_vec, head_vec, sel(SSD_HEAD_DIM), sel(L), par_g(1), par_g(1)],
        out_specs=pl.BlockSpec((L, gsw), lambda b, g, s: (b * steps + s, g)),
        out_shape=jax.ShapeDtypeStruct((m, d_model), BF16),
        scratch_shapes=[pltpu.VMEM((gs, N, gw), F32),
                        pltpu.VMEM((gsw // LANES, halo, LANES), F32),
                        pltpu.VMEM((gsn // LANES, halo, LANES), F32),
                        pltpu.VMEM((gsn // LANES, halo, LANES), F32),
                        pltpu.VMEM((gsw // LANES, halo + L, LANES), F32),
                        pltpu.VMEM((gsn // LANES, halo + L, LANES), F32),
                        pltpu.VMEM((gsn // LANES, halo + L, LANES), F32)],
        compiler_params=_cparams(("arbitrary", "arbitrary", "arbitrary")),
        name="ssd",
    )(pb, pb, pb, pb, dt_raw,
      conv_w, conv_w, conv_w, cb2, cb2, cb2,
      pad_heads(dt_bias), pad_heads(a_log),
      select(SSD_HEAD_DIM), select(L), rep_heads(d_skip), b_norm.reshape(1, d_model))


def _conf_tail_kernel(c_ref, dww_ref, dwb_ref, lng_ref, lnb_ref, w2_hbm, b2_ref,
                      h_ref, gpost_ref, gnext_ref, hout_ref, unext_ref,
                      carry_ref, ext_ref, conv_ref, w2_ref, stage_ref, sem_ref,
                      *, row_block, layer):
    tm, d = c_ref.shape
    halo = CONF_HALO
    first = halo - (CONF_KERNEL - 1)
    n_tiles = d // LANES

    @pl.when((pl.program_id(0) == 0) & (pl.program_id(1) == 0))
    def _():
        _load_weight_bf16(w2_hbm, layer, w2_ref, stage_ref, sem_ref)

    @pl.when(pl.program_id(1) == 0)
    def _():
        carry_ref[...] = jnp.zeros_like(carry_ref)

    for j in range(n_tiles):
        ln = slice(j * LANES, (j + 1) * LANES)
        ext_ref[j, 0:halo, :] = carry_ref[j]
        ext_ref[j, halo:halo + tm, :] = c_ref[:, ln]
        carry_ref[j] = c_ref[tm - halo:tm, ln]

    def col_tile(j, carry):
        cj = pl.ds(pl.multiple_of(j * LANES, LANES), LANES)
        for rb in range(tm // row_block):
            base = rb * row_block
            acc = dwb_ref[:, cj] + dww_ref[0:1, cj] * ext_ref[j, pl.ds(base + first, row_block), :]
            for k in range(1, CONF_KERNEL):
                acc = acc + dww_ref[k:k + 1, cj] * ext_ref[j, pl.ds(base + first + k, row_block), :]
            conv_ref[base:base + row_block, cj] = acc
        return carry

    lax.fori_loop(0, n_tiles, col_tile, 0)

    x = conv_ref[...]
    mu = jnp.mean(x, axis=-1, keepdims=True)
    xc = x - mu
    var = jnp.mean(xc * xc, axis=-1, keepdims=True)
    y = _silu(xc * lax.rsqrt(var + LN_EPS) * lng_ref[...] + lnb_ref[...])
    m = _dot(y.astype(BF16), w2_ref[...]) + b2_ref[...]
    _residual_epilogue(m, h_ref[...], gpost_ref[...], gnext_ref[...], hout_ref, unext_ref)


def conf_tail(c, dw_w, dw_b, ln_g, ln_b, w2_stack, layer, b2, h, gpost, gnext,
              batch, seq, tm=256, row_block=64, stage_rows=WEIGHT_STAGE_ROWS):
    m, d = c.shape
    tm = min(tm, seq)
    row_block = min(row_block, tm)
    stage_rows = min(stage_rows, d)
    steps = seq // tm
    row = pl.BlockSpec((tm, d), lambda b, s: (b * steps + s, 0))
    vec = pl.BlockSpec((1, d), lambda b, s: (0, 0))
    dww = jnp.pad(dw_w, ((0, CONF_HALO - CONF_KERNEL), (0, 0)))
    return pl.pallas_call(
        functools.partial(_conf_tail_kernel, row_block=row_block, layer=layer),
        grid=(batch, steps),
        in_specs=[row, pl.BlockSpec((CONF_HALO, d), lambda b, s: (0, 0)), vec, vec, vec,
                  pl.BlockSpec(memory_space=pl.ANY),
                  vec, row, vec, vec],
        out_specs=[row, row],
        out_shape=[jax.ShapeDtypeStruct((m, d), F32), jax.ShapeDtypeStruct((m, d), BF16)],
        scratch_shapes=[pltpu.VMEM((d // LANES, CONF_HALO, LANES), F32),
                        pltpu.VMEM((d // LANES, CONF_HALO + tm, LANES), F32),
                        pltpu.VMEM((tm, d), F32),
                        pltpu.VMEM((d, d), BF16), pltpu.VMEM((2, stage_rows, d), F32),
                        pltpu.SemaphoreType.DMA((2,))],
        compiler_params=_cparams(("arbitrary", "arbitrary")),
        name="conf_tail",
    )(c, dww, dw_b.reshape(1, d), ln_g.reshape(1, d), ln_b.reshape(1, d), w2_stack,
      b2.reshape(1, d), h, gpost.reshape(1, d), gnext.reshape(1, d))


def kernel(x, mix_pre_g, mix_post_g, ffn_pre_g, ffn_post_g, hgrn_lb_logits, even_w_in,
           hgrn_norm_g, ssd_conv_w, ssd_conv_b, ssd_dt_bias, ssd_a_log, ssd_d, ssd_norm_g,
           even_w_out, conf_w1, conf_b1, conf_dw_w, conf_dw_b, conf_ln_g, conf_ln_b,
           conf_w2, conf_b2, ffn_w_gate, ffn_w_up, ffn_w_down):
    batch, seq, d = x.shape
    depth = mix_pre_g.shape[0]
    hidden = ffn_w_gate.shape[2]
    m = batch * seq
    tn = 512
    tn_in = min(1024, d)
    main_cols = 6 * d + 2 * SSD_GROUPS * SSD_STATE
    n_ssd_heads = d // SSD_HEAD_DIM
    f_blocks = d // tn_in

    lb_p = jax.nn.softmax(hgrn_lb_logits.astype(F32), axis=0)
    lower_bounds = jnp.cumsum(lb_p, axis=0) - lb_p[0]

    w_in_t = jnp.swapaxes(even_w_in, 1, 2)
    conf_b1_3d = conf_b1.reshape(conf_b1.shape[0], 1, -1)
    plus = lambda off: (lambda n: n + off)

    h = x.reshape(m, d)
    u = prenorm(h, mix_pre_g[0])
    for layer in range(depth):
        i = layer // 2
        if layer % 2 == 0:
            skip_f = lambda n: n + jnp.where(n >= f_blocks, f_blocks, 0)
            pb = wide_proj(u, [w_in_t], i, [skip_f], [], [], (main_cols - d) // tn_in,
                           _identity_epilogue, BF16, "even_in_proj", tn=tn_in,
                           w_is_transposed=True)
            f_pre = wide_proj(u, [w_in_t], i, [plus(f_blocks)], [], [], f_blocks,
                              _identity_epilogue, F32, "even_f_proj", tn=tn_in,
                              w_is_transposed=True)
            dt_raw = small_proj(u, w_in_t, i, main_cols, n_ssd_heads)
            o_a = hgrn_mix(pb, f_pre, lower_bounds[i], hgrn_norm_g[i], batch, seq, d)
            o_b = ssd_mix(pb, dt_raw, ssd_conv_w[i], ssd_conv_b[i], ssd_dt_bias[i],
                          ssd_a_log[i], ssd_d[i], ssd_norm_g[i], batch, seq, d)
            h, u = out_proj([o_a, o_b], even_w_out, i, h, mix_post_g[layer],
                            ffn_pre_g[layer], True, "even_out_proj")
        else:
            c = wide_proj(u, [conf_w1, conf_w1], i, [plus(0), plus(d // tn)],
                          [conf_b1_3d, conf_b1_3d], [plus(0), plus(d // tn)], d // tn,
                          _glu_epilogue, F32, "conf_glu", tn=tn)
            h, u = conf_tail(c, conf_dw_w[i], conf_dw_b[i], conf_ln_g[i], conf_ln_b[i],
                             conf_w2, i, conf_b2[i], h, mix_post_g[layer],
                             ffn_pre_g[layer], batch, seq)
        act = wide_proj(u, [ffn_w_gate, ffn_w_up], layer, [plus(0), plus(0)], [], [],
                        hidden // tn, _swiglu_epilogue, BF16, "ffn_in", tn=tn)
        last = layer == depth - 1
        gnext = mix_pre_g[layer] if last else mix_pre_g[layer + 1]
        h, u = out_proj([act], ffn_w_down, layer, h, ffn_post_g[layer], gnext, not last,
                        "ffn_out")
    return h.reshape(batch, seq, d)
```

```python
import functools

import jax
import jax.numpy as jnp
from jax import lax
from jax.experimental import pallas as pl
from jax.experimental.pallas import tpu as pltpu

F32 = jnp.float32
BF16 = jnp.bfloat16

LOG2_E = 1.4426950408889634
RMS_EPS = 1e-6
LN_EPS = 1e-5
HGRN_F_MIN = 1e-6
HGRN_HEAD = 128
HGRN_CHUNK = 64
HGRN_SUB = 8
HGRN_HEADS_PER_PART = 4
HGRN_SEQS_PER_STEP = 2
HGRN_PARTS_PER_STEP = 4
SSD_HEAD_DIM = 64
SSD_GROUPS = 4
SSD_STATE = 128
SSD_CONV = 4
SSD_CHUNK = 128
SSD_GROUPS_PER_STEP = 4
CONF_KERNEL = 31
CONF_HALO = 32
WEIGHT_STAGE_ROWS = 512
LANES = 128
VMEM_LIMIT = 56 * 1024 * 1024


def _cparams(semantics):
    return pltpu.CompilerParams(dimension_semantics=semantics,
                                vmem_limit_bytes=VMEM_LIMIT)


def _sigmoid(x):
    return 0.5 * jnp.tanh(0.5 * x) + 0.5


def _silu(x):
    t = 0.5 * x
    return t * jnp.tanh(t) + t


def _softplus(x):
    return jnp.maximum(x, 0.0) + jnp.log1p(jnp.exp(-jnp.abs(x)))


def _rms_rows(x, g, eps=RMS_EPS):
    ms = jnp.mean(x * x, axis=-1, keepdims=True)
    return x * lax.rsqrt(ms + eps) * g


def _dot(a, b):
    return jnp.dot(a, b, preferred_element_type=F32)


def _dot_nt(a, b):
    return lax.dot_general(a, b, (((1,), (1,)), ((), ())), preferred_element_type=F32)


def _dot_tn(a, b):
    return lax.dot_general(a, b, (((0,), (0,)), ((), ())), preferred_element_type=F32)


def _split3(x):
    hi = x.astype(BF16)
    r1 = x - hi.astype(F32)
    mid = r1.astype(BF16)
    lo = (r1 - mid.astype(F32)).astype(BF16)
    return hi, mid, lo


def _dot_exact_rhs(sel, x):
    return _dot(jnp.concatenate([sel, sel, sel], axis=1),
                jnp.concatenate(_split3(x), axis=0))


def _lane_stack3(x, n):
    hi, mid, lo = _split3(x)
    return (hi.astype(F32) + pltpu.roll(mid.astype(F32), n, 1)
            + pltpu.roll(lo.astype(F32), 2 * n, 1)).astype(BF16)


def _lower_tri(n, dtype):
    r = lax.broadcasted_iota(jnp.int32, (n, n), 0)
    c = lax.broadcasted_iota(jnp.int32, (n, n), 1)
    return (r >= c).astype(dtype)


def _prenorm_kernel(x_ref, g_ref, o_ref):
    o_ref[...] = _rms_rows(x_ref[...], g_ref[...]).astype(o_ref.dtype)


def prenorm(x, g, tm=512):
    m, d = x.shape
    tm = min(tm, m)
    return pl.pallas_call(
        _prenorm_kernel,
        grid=(m // tm,),
        in_specs=[pl.BlockSpec((tm, d), lambda i: (i, 0)),
                  pl.BlockSpec((1, d), lambda i: (0, 0))],
        out_specs=pl.BlockSpec((tm, d), lambda i: (i, 0)),
        out_shape=jax.ShapeDtypeStruct((m, d), BF16),
        compiler_params=_cparams(("arbitrary",)),
        name="prenorm",
    )(x, g.reshape(1, d))


def _wide_kernel(*refs, n_w, n_b, epilogue, w_is_transposed):
    u_ref = refs[0]
    w_refs = refs[1:1 + n_w]
    b_refs = refs[1 + n_w:1 + n_w + n_b]
    o_ref = refs[1 + n_w + n_b]
    s_refs = refs[2 + n_w + n_b:]

    @pl.when(pl.program_id(1) == 0)
    def _():
        for w, s in zip(w_refs, s_refs):
            s[...] = w[...].astype(BF16)

    u = u_ref[...]
    dot = _dot_nt if w_is_transposed else _dot
    ys = [dot(u, s[...]) for s in s_refs]
    o_ref[...] = epilogue(*ys, *[b[...] for b in b_refs]).astype(o_ref.dtype)


def wide_proj(u, ws, layer, w_col_maps, bs, b_col_maps, n_blocks, epilogue, out_dtype,
              name, tn=512, tm=1024, w_is_transposed=False):
    m, k = u.shape
    tm = min(tm, m)
    in_specs = [pl.BlockSpec((tm, k), lambda n, i: (i, 0))]
    for cmap in w_col_maps:
        if w_is_transposed:
            in_specs.append(pl.BlockSpec((None, tn, k), lambda n, i, cmap=cmap: (layer, cmap(n), 0)))
        else:
            in_specs.append(pl.BlockSpec((None, k, tn), lambda n, i, cmap=cmap: (layer, 0, cmap(n))))
    for cmap in b_col_maps:
        in_specs.append(pl.BlockSpec((None, 1, tn), lambda n, i, cmap=cmap: (layer, 0, cmap(n))))
    return pl.pallas_call(
        functools.partial(_wide_kernel, n_w=len(ws), n_b=len(bs), epilogue=epilogue,
                          w_is_transposed=w_is_transposed),
        grid=(n_blocks, m // tm),
        in_specs=in_specs,
        out_specs=pl.BlockSpec((tm, tn), lambda n, i: (i, n)),
        out_shape=jax.ShapeDtypeStruct((m, n_blocks * tn), out_dtype),
        scratch_shapes=[pltpu.VMEM((tn, k) if w_is_transposed else (k, tn), BF16) for _ in ws],
        compiler_params=_cparams(("arbitrary", "arbitrary")),
        name=name,
    )(u, *ws, *bs)


def _identity_epilogue(y):
    return y


def _swiglu_epilogue(g, up):
    return _silu(g) * up


def _glu_epilogue(a, g, ba, bg):
    return (a + ba) * _sigmoid(g + bg)


def _small_proj_kernel(u_ref, w_ref, o_ref):
    n, k = w_ref.shape
    w = jnp.concatenate([w_ref[...], jnp.zeros((LANES - n, k), F32)], axis=0)
    o_ref[...] = _dot_nt(u_ref[...], w.astype(BF16))


def small_proj(u, w_t_stack, layer, first_row, n, tm=1024):
    m, k = u.shape
    tm = min(tm, m)
    return pl.pallas_call(
        _small_proj_kernel,
        grid=(m // tm,),
        in_specs=[pl.BlockSpec((tm, k), lambda i: (i, 0)),
                  pl.BlockSpec((None, n, k), lambda i: (layer, first_row // n, 0))],
        out_specs=pl.BlockSpec((tm, LANES), lambda i: (i, 0)),
        out_shape=jax.ShapeDtypeStruct((m, LANES), F32),
        compiler_params=_cparams(("arbitrary",)),
        name="dt_proj",
    )(u, w_t_stack)


def _residual_epilogue(m, h, gpost, gnext, hout_ref, unext_ref):
    hn = h + _rms_rows(m, gpost)
    hout_ref[...] = hn
    if unext_ref is not None:
        unext_ref[...] = _rms_rows(hn, gnext).astype(unext_ref.dtype)


def _load_weight_bf16(w_hbm, layer, w16_ref, stage_ref, sem_ref):
    rows = stage_ref.shape[1]
    n_chunks = w16_ref.shape[0] // rows

    def chunk_copy(c):
        return pltpu.make_async_copy(w_hbm.at[layer, pl.ds(c * rows, rows), :],
                                     stage_ref.at[c % 2], sem_ref.at[c % 2])

    chunk_copy(0).start()
    for c in range(n_chunks):
        if c + 1 < n_chunks:
            chunk_copy(c + 1).start()
        chunk_copy(c).wait()
        w16_ref[c * rows:(c + 1) * rows, :] = stage_ref[c % 2].astype(BF16)


def _out_proj_kernel(*refs, n_a, with_next, layer):
    a_refs = refs[:n_a]
    w_hbm, h_ref, gpost_ref, gnext_ref, hout_ref = refs[n_a:n_a + 5]
    unext_ref = refs[n_a + 5] if with_next else None
    w16_ref, stage_ref, sem_ref = refs[-3:]

    @pl.when(pl.program_id(0) == 0)
    def _():
        _load_weight_bf16(w_hbm, layer, w16_ref, stage_ref, sem_ref)

    a = jnp.concatenate([a_ref[...] for a_ref in a_refs], axis=1)
    m = _dot(a, w16_ref[...])
    _residual_epilogue(m, h_ref[...], gpost_ref[...], gnext_ref[...], hout_ref, unext_ref)


def out_proj(acts, w_stack, layer, h, gpost, gnext, with_next, name, tm=256,
             stage_rows=WEIGHT_STAGE_ROWS):
    m = h.shape[0]
    _, kk, d = w_stack.shape
    tm = min(tm, m)
    stage_rows = min(stage_rows, kk)
    row = pl.BlockSpec((tm, d), lambda i: (i, 0))
    vec = pl.BlockSpec((1, d), lambda i: (0, 0))
    out_shape = [jax.ShapeDtypeStruct((m, d), F32)]
    out_specs = [row]
    if with_next:
        out_shape.append(jax.ShapeDtypeStruct((m, d), BF16))
        out_specs.append(row)
    res = pl.pallas_call(
        functools.partial(_out_proj_kernel, n_a=len(acts), with_next=with_next, layer=layer),
        grid=(m // tm,),
        in_specs=[pl.BlockSpec((tm, a.shape[1]), lambda i: (i, 0)) for a in acts] + [
            pl.BlockSpec(memory_space=pl.ANY), row, vec, vec],
        out_specs=out_specs,
        out_shape=out_shape,
        scratch_shapes=[pltpu.VMEM((kk, d), BF16), pltpu.VMEM((2, stage_rows, d), F32),
                        pltpu.SemaphoreType.DMA((2,))],
        compiler_params=_cparams(("arbitrary",)),
        name=name,
    )(*acts, w_stack, h, gpost.reshape(1, d), gnext.reshape(1, d))
    return (res[0], res[1]) if with_next else (res[0], None)


def _hgrn_kernel(q_ref, f_ref, v_ref, gate_ref, lb_ref, an_ref, o_ref, st_ref,
                 *, n_chunks, n_heads, part_heads, n_seq):
    C, SUB, HD = HGRN_CHUNK, HGRN_SUB, HGRN_HEAD
    n_sub = C // SUB
    mid = SUB // 2 - 1
    width = part_heads * HD
    head_lanes = [slice(h * HD, (h + 1) * HD) for h in range(part_heads)]
    parts = [slice(p * width, (p + 1) * width) for p in range(n_heads // part_heads)]

    @pl.when(pl.program_id(2) == 0)
    def _():
        st_ref[...] = jnp.zeros_like(st_ref)

    tri = _lower_tri(C, BF16)
    causal = (lax.broadcasted_iota(jnp.int32, (C, C), 0)
              >= lax.broadcasted_iota(jnp.int32, (C, C), 1))
    def gates(sl, bi, part):
        lb = lb_ref[:, part]
        f = (0.5 + 0.5 * lb) + (0.5 - 0.5 * lb) * jnp.tanh(0.5 * f_ref[bi, sl, part])
        k = 1.0 - f
        log2_f = jnp.log(jnp.maximum(f, HGRN_F_MIN)) * LOG2_E
        qs = _silu(q_ref[bi, sl, part].astype(F32))
        b = _dot_exact_rhs(tri, log2_f)
        return qs, k, b, v_ref[bi, sl, part]

    def scores_and_state(vals, h0):
        qs, k, b, v16 = vals
        b_last = b[C - 1:C, :]

        q_in16 = (qs * jnp.exp2(b)).astype(BF16)
        k_end16 = (k * jnp.exp2(b_last - b)).astype(BF16)
        dec_last = jnp.exp2(b_last)
        sts = [st_ref[h0 + h] for h in range(part_heads)]
        o_heads = [_dot(q_in16[:, s], st.astype(BF16)) for s, st in zip(head_lanes, sts)]
        for h, (s, st) in enumerate(zip(head_lanes, sts)):
            row_decay = jnp.broadcast_to(dec_last[:, s], (HD, HD)).T
            st_ref[h0 + h] = st * row_decay + _dot_tn(k_end16[:, s], v16[:, s])

        blocks = [[] for _ in range(part_heads)]
        keys = []
        prev_ref = None
        for i in range(n_sub):
            rows = slice(i * SUB, (i + 1) * SUB)
            ref_b = b[i * SUB + mid:i * SUB + mid + 1, :]
            if keys:
                step = jnp.exp2(ref_b - prev_ref)
                keys = [kj * step for kj in keys]
            keys.append(k[rows, :] * jnp.exp2(ref_b - b[rows, :]))
            prev_ref = ref_b
            qt16 = (qs[rows, :] * jnp.exp2(b[rows, :] - ref_b)).astype(BF16)
            pad = [jnp.zeros(((n_sub - 1 - i) * SUB, width), F32)] if i < n_sub - 1 else []
            kt16 = jnp.concatenate(keys + pad, axis=0).astype(BF16)
            for h, s in enumerate(head_lanes):
                blocks[h].append(_dot_nt(qt16[:, s], kt16[:, s]))
        return o_heads, blocks

    def outputs(sl, bi, part, vals, o_heads, blocks):
        v16 = vals[3]
        for h, s in enumerate(head_lanes):
            scores = jnp.where(causal, jnp.concatenate(blocks[h], axis=0), 0.0)
            o_heads[h] = o_heads[h] + _dot(scores.astype(BF16), v16[:, s])
        y = jnp.concatenate(
            [o * lax.rsqrt(jnp.mean(o * o, axis=-1, keepdims=True) + RMS_EPS) for o in o_heads],
            axis=1)
        y = y * an_ref[:, part] * _silu(gate_ref[bi, sl, part].astype(F32))
        o_ref[bi, sl, part] = y.astype(o_ref.dtype)

    def chunk(c, carry):
        sl = pl.ds(pl.multiple_of(c * C, C), C)
        units = [(bi, part) for bi in range(n_seq) for part in parts]
        vals = [gates(sl, bi, part) for bi, part in units]
        mids = {}
        for p in range(len(units) + 1):
            if p < len(units):
                mids[p] = scores_and_state(vals[p], p * part_heads)
            if p >= 1:
                outputs(sl, *units[p - 1], vals[p - 1], *mids[p - 1])
        return carry

    lax.fori_loop(0, n_chunks, chunk, 0)


def hgrn_mix(pb, f_pre, lb, a_norm, batch, seq, d_model, rows_per_step=256):
    n_heads = d_model // HGRN_HEAD
    part_heads = min(HGRN_HEADS_PER_PART, n_heads)
    hb = min(HGRN_PARTS_PER_STEP * part_heads, n_heads)
    gw = hb * HGRN_HEAD
    n_groups = n_heads // hb
    n_seq = min(HGRN_SEQS_PER_STEP, batch)
    cs = min(rows_per_step, seq)
    m = batch * seq

    def col(seg):
        return pl.BlockSpec((n_seq, cs, gw), lambda b, h, s, seg=seg: (b, s, seg * n_groups + h))

    vec = pl.BlockSpec((1, gw), lambda b, h, s: (0, h))
    pb3 = pb.reshape(batch, seq, -1)
    out = pl.pallas_call(
        functools.partial(_hgrn_kernel, n_chunks=cs // HGRN_CHUNK, n_heads=hb,
                          part_heads=part_heads, n_seq=n_seq),
        grid=(batch // n_seq, n_groups, seq // cs),
        in_specs=[col(0), col(0), col(1), col(2), vec, vec],
        out_specs=col(0),
        out_shape=jax.ShapeDtypeStruct((batch, seq, d_model), BF16),
        scratch_shapes=[pltpu.VMEM((n_seq * hb, HGRN_HEAD, HGRN_HEAD), F32)],
        compiler_params=_cparams(("arbitrary", "arbitrary", "arbitrary")),
        name="hgrn2",
    )(pb3, f_pre.reshape(batch, seq, d_model), pb3, pb3,
      lb.reshape(1, d_model), a_norm.reshape(1, d_model))
    return out.reshape(m, d_model)


def _causal_conv(raw_ref, carry_ref, ext_ref, w_ref, b_ref, n_rows, n_taps, halo):
    first = halo - n_taps + 1
    outs = []
    for j in range(raw_ref.shape[1] // LANES):
        ln = slice(j * LANES, (j + 1) * LANES)
        raw = raw_ref[:, ln].astype(F32)
        ext_ref[j, 0:halo, :] = carry_ref[j]
        ext_ref[j, halo:halo + n_rows, :] = raw
        carry_ref[j] = raw[n_rows - halo:n_rows, :]
        acc = b_ref[:, ln] + w_ref[0:1, ln] * ext_ref[j, pl.ds(first, n_rows), :]
        for k in range(1, n_taps):
            acc = acc + w_ref[k:k + 1, ln] * ext_ref[j, pl.ds(first + k, n_rows), :]
        outs.append(acc)
    return jnp.concatenate(outs, axis=1)


def _ssd_kernel(x_ref, bm_ref, cm_ref, z_ref, dt_ref,
                cwx_ref, cwb_ref, cwc_ref, cbx_ref, cbb_ref, cbc_ref,
                dtb_ref, alog_ref, selp_ref, selw_ref, dskip_ref, norm_ref, o_ref,
                st_ref, carx_ref, carb_ref, carc_ref, extx_ref, extb_ref, extc_ref,
                *, heads_per_group, n_groups, n_heads):
    L, P, R, N = SSD_CHUNK, SSD_HEAD_DIM, heads_per_group, SSD_STATE
    gw = R * P
    groups = range(n_groups)
    g_lanes = [slice(j * gw, (j + 1) * gw) for j in groups]
    n_lanes = [slice(j * N, (j + 1) * N) for j in groups]

    @pl.when(pl.program_id(2) == 0)
    def _():
        st_ref[...] = jnp.zeros_like(st_ref)
        carx_ref[...] = jnp.zeros_like(carx_ref)
        carb_ref[...] = jnp.zeros_like(carb_ref)
        carc_ref[...] = jnp.zeros_like(carc_ref)

    halo = carx_ref.shape[1]
    xc = _silu(_causal_conv(x_ref, carx_ref, extx_ref, cwx_ref, cbx_ref, L, SSD_CONV, halo))
    bc16 = _silu(_causal_conv(bm_ref, carb_ref, extb_ref, cwb_ref, cbb_ref, L, SSD_CONV,
                              halo)).astype(BF16)
    cc16 = _silu(_causal_conv(cm_ref, carc_ref, extc_ref, cwc_ref, cbc_ref, L, SSD_CONV,
                              halo)).astype(BF16)

    lane = lax.broadcasted_iota(jnp.int32, (L, LANES), 1)
    dt = jnp.where(lane < n_heads, _softplus(dt_ref[...] + dtb_ref[...]), 0.0)
    dta = dt * (-LOG2_E * jnp.exp(alog_ref[...]))
    cs = _dot_exact_rhs(_lower_tri(L, BF16), dta)

    dt3 = _lane_stack3(dt, n_heads)
    cs3 = _lane_stack3(cs, n_heads)
    dt_x = [_dot(dt3, selp_ref[j]) for j in groups]
    cs_x = [_dot(cs3, selp_ref[j]) for j in groups]
    cs_w = [_dot(cs3, selw_ref[j]) for j in groups]

    rr = lax.broadcasted_iota(jnp.int32, (L, L), 0)
    cl = lax.broadcasted_iota(jnp.int32, (L, L), 1)
    causal = rr >= cl
    ys, xdt16s, cbs = [], [], []
    for j in groups:
        xdt = xc[:, g_lanes[j]] * dt_x[j]
        cs_last = cs_x[j][L - 1:L, :]
        st = st_ref[j]
        ys.append(_dot(cc16[:, n_lanes[j]], st.astype(BF16)) * jnp.exp2(cs_x[j]))
        st_ref[j] = st * jnp.exp2(cs_last) + _dot_tn(
            bc16[:, n_lanes[j]], (xdt * jnp.exp2(cs_last - cs_x[j])).astype(BF16))
        cb = _dot_nt(cc16[:, n_lanes[j]], bc16[:, n_lanes[j]])
        cbs.append(jnp.where(causal, cb, 0.0))
        xdt16s.append(xdt.astype(BF16))

    parts = [[] for _ in groups]
    for r in range(R):
        for j in groups:
            col = cs_w[j][:, r * L:(r + 1) * L]
            seg = col - col.T
            decay = jnp.exp2(jnp.minimum(seg, 0.0))
            parts[j].append(_dot((cbs[j] * decay).astype(BF16),
                                 xdt16s[j][:, r * P:(r + 1) * P]))

    for j in groups:
        ln = g_lanes[j]
        y = ys[j] + jnp.concatenate(parts[j], axis=1) + dskip_ref[:, ln] * xc[:, ln]
        yz = y * _silu(z_ref[:, ln].astype(F32))
        o_ref[:, ln] = _rms_rows(yz, norm_ref[:, ln]).astype(o_ref.dtype)


def ssd_mix(pb, dt_raw, conv_w, conv_b, dt_bias, a_log, d_skip, b_norm,
            batch, seq, d_model):
    L, G, N = SSD_CHUNK, SSD_GROUPS, SSD_STATE
    n_heads = d_model // SSD_HEAD_DIM
    R = n_heads // G
    gw = d_model // G
    steps = seq // L
    m = batch * seq
    halo = 8
    gs = SSD_GROUPS_PER_STEP
    gsw, gsn = gs * gw, gs * N
    bn0 = 5 * d_model // gsn
    cwb0 = d_model // gsn

    def rep_heads(v):
        return jnp.repeat(v.astype(F32), SSD_HEAD_DIM).reshape(1, d_model)

    def pad_heads(v):
        return jnp.pad(v.astype(F32), (0, LANES - n_heads)).reshape(1, LANES)

    def select(width):
        row = jnp.arange(LANES, dtype=jnp.int32)[None, :, None]
        lane = jnp.arange(R * width, dtype=jnp.int32)[None, None, :]
        grp = jnp.arange(G, dtype=jnp.int32)[:, None, None]
        hit = (row % n_heads == grp * R + lane // width) & (row < 3 * n_heads)
        return hit.astype(BF16)

    row_g = lambda off: pl.BlockSpec((L, gsw), lambda b, g, s, off=off: (b * steps + s, off + g))
    row_n = lambda off: pl.BlockSpec((L, gsn), lambda b, g, s, off=off: (b * steps + s, off + g))
    par_g = lambda rows: pl.BlockSpec((rows, gsw), lambda b, g, s: (0, g))
    par_n = lambda rows, off: pl.BlockSpec((rows, gsn), lambda b, g, s, off=off: (0, off + g))
    head_vec = pl.BlockSpec((1, LANES), lambda b, g, s: (0, 0))
    sel = lambda width: pl.BlockSpec((gs, LANES, R * width), lambda b, g, s: (g, 0, 0))
    cb2 = conv_b.reshape(1, -1)
    n_steps_g = G // gs

    return pl.pallas_call(
        functools.partial(_ssd_kernel, heads_per_group=R, n_groups=gs, n_heads=n_heads),
        grid=(batch, n_steps_g, steps),
        in_specs=[row_g(4 * n_steps_g), row_n(bn0), row_n(bn0 + n_steps_g), row_g(3 * n_steps_g),
                  pl.BlockSpec((L, LANES), lambda b, g, s: (b * steps + s, 0)),
                  par_g(SSD_CONV), par_n(SSD_CONV, cwb0), par_n(SSD_CONV, cwb0 + n_steps_g),
                  par_g(1), par_n(1, cwb0), par_n(1, cwb0 + n_steps_g),
                  head_vec, head_vec, sel(SSD_HEAD_DIM), sel(L), par_g(1), par_g(1)],
        out_specs=pl.BlockSpec((L, gsw), lambda b, g, s: (b * steps + s, g)),
        out_shape=jax.ShapeDtypeStruct((m, d_model), BF16),
        scratch_shapes=[pltpu.VMEM((gs, N, gw), F32),
                        pltpu.VMEM((gsw // LANES, halo, LANES), F32),
                        pltpu.VMEM((gsn // LANES, halo, LANES), F32),
                        pltpu.VMEM((gsn // LANES, halo, LANES), F32),
                        pltpu.VMEM((gsw // LANES, halo + L, LANES), F32),
                        pltpu.VMEM((gsn // LANES, halo + L, LANES), F32),
                        pltpu.VMEM((gsn // LANES, halo + L, LANES), F32)],
        compiler_params=_cparams(("arbitrary", "arbitrary", "arbitrary")),
        name="ssd",
    )(pb, pb, pb, pb, dt_raw,
      conv_w, conv_w, conv_w, cb2, cb2, cb2,
      pad_heads(dt_bias), pad_heads(a_log),
      select(SSD_HEAD_DIM), select(L), rep_heads(d_skip), b_norm.reshape(1, d_model))


def _conf_tail_kernel(c_ref, dww_ref, dwb_ref, lng_ref, lnb_ref, w2_hbm, b2_ref,
                      h_ref, gpost_ref, gnext_ref, hout_ref, unext_ref,
                      carry_ref, ext_ref, conv_ref, w2_ref, stage_ref, sem_ref,
                      *, row_block, layer):
    tm, d = c_ref.shape
    halo = CONF_HALO
    first = halo - (CONF_KERNEL - 1)
    n_tiles = d // LANES

    @pl.when((pl.program_id(0) == 0) & (pl.program_id(1) == 0))
    def _():
        _load_weight_bf16(w2_hbm, layer, w2_ref, stage_ref, sem_ref)

    @pl.when(pl.program_id(1) == 0)
    def _():
        carry_ref[...] = jnp.zeros_like(carry_ref)

    for j in range(n_tiles):
        ln = slice(j * LANES, (j + 1) * LANES)
        ext_ref[j, 0:halo, :] = carry_ref[j]
        ext_ref[j, halo:halo + tm, :] = c_ref[:, ln]
        carry_ref[j] = c_ref[tm - halo:tm, ln]

    def col_tile(j, carry):
        cj = pl.ds(pl.multiple_of(j * LANES, LANES), LANES)
        for rb in range(tm // row_block):
            base = rb * row_block
            acc = dwb_ref[:, cj] + dww_ref[0:1, cj] * ext_ref[j, pl.ds(base + first, row_block), :]
            for k in range(1, CONF_KERNEL):
                acc = acc + dww_ref[k:k + 1, cj] * ext_ref[j, pl.ds(base + first + k, row_block), :]
            conv_ref[base:base + row_block, cj] = acc
        return carry

    lax.fori_loop(0, n_tiles, col_tile, 0)

    x = conv_ref[...]
    mu = jnp.mean(x, axis=-1, keepdims=True)
    xc = x - mu
    var = jnp.mean(xc * xc, axis=-1, keepdims=True)
    y = _silu(xc * lax.rsqrt(var + LN_EPS) * lng_ref[...] + lnb_ref[...])
    m = _dot(y.astype(BF16), w2_ref[...]) + b2_ref[...]
    _residual_epilogue(m, h_ref[...], gpost_ref[...], gnext_ref[...], hout_ref, unext_ref)


def conf_tail(c, dw_w, dw_b, ln_g, ln_b, w2_stack, layer, b2, h, gpost, gnext,
              batch, seq, tm=512, row_block=64, stage_rows=WEIGHT_STAGE_ROWS // 2):
    m, d = c.shape
    tm = min(tm, seq)
    row_block = min(row_block, tm)
    stage_rows = min(stage_rows, d)
    steps = seq // tm
    row = pl.BlockSpec((tm, d), lambda b, s: (b * steps + s, 0))
    vec = pl.BlockSpec((1, d), lambda b, s: (0, 0))
    dww = jnp.pad(dw_w, ((0, CONF_HALO - CONF_KERNEL), (0, 0)))
    return pl.pallas_call(
        functools.partial(_conf_tail_kernel, row_block=row_block, layer=layer),
        grid=(batch, steps),
        in_specs=[row, pl.BlockSpec((CONF_HALO, d), lambda b, s: (0, 0)), vec, vec, vec,
                  pl.BlockSpec(memory_space=pl.ANY),
                  vec, row, vec, vec],
        out_specs=[row, row],
        out_shape=[jax.ShapeDtypeStruct((m, d), F32), jax.ShapeDtypeStruct((m, d), BF16)],
        scratch_shapes=[pltpu.VMEM((d // LANES, CONF_HALO, LANES), F32),
                        pltpu.VMEM((d // LANES, CONF_HALO + tm, LANES), F32),
                        pltpu.VMEM((tm, d), F32),
                        pltpu.VMEM((d, d), BF16), pltpu.VMEM((2, stage_rows, d), F32),
                        pltpu.SemaphoreType.DMA((2,))],
        compiler_params=_cparams(("arbitrary", "arbitrary")),
        name="conf_tail",
    )(c, dww, dw_b.reshape(1, d), ln_g.reshape(1, d), ln_b.reshape(1, d), w2_stack,
      b2.reshape(1, d), h, gpost.reshape(1, d), gnext.reshape(1, d))


def kernel(x, mix_pre_g, mix_post_g, ffn_pre_g, ffn_post_g, hgrn_lb_logits, even_w_in,
           hgrn_norm_g, ssd_conv_w, ssd_conv_b, ssd_dt_bias, ssd_a_log, ssd_d, ssd_norm_g,
           even_w_out, conf_w1, conf_b1, conf_dw_w, conf_dw_b, conf_ln_g, conf_ln_b,
           conf_w2, conf_b2, ffn_w_gate, ffn_w_up, ffn_w_down):
    batch, seq, d = x.shape
    depth = mix_pre_g.shape[0]
    hidden = ffn_w_gate.shape[2]
    m = batch * seq
    tn = 512
    tn_in = min(1024, d)
    main_cols = 6 * d + 2 * SSD_GROUPS * SSD_STATE
    n_ssd_heads = d // SSD_HEAD_DIM
    f_blocks = d // tn_in

    lb_p = jax.nn.softmax(hgrn_lb_logits.astype(F32), axis=0)
    lower_bounds = jnp.cumsum(lb_p, axis=0) - lb_p[0]

    w_in_t = jnp.swapaxes(even_w_in, 1, 2)
    conf_b1_3d = conf_b1.reshape(conf_b1.shape[0], 1, -1)
    plus = lambda off: (lambda n: n + off)

    h = x.reshape(m, d)
    u = prenorm(h, mix_pre_g[0])
    for layer in range(depth):
        i = layer // 2
        if layer % 2 == 0:
            skip_f = lambda n: n + jnp.where(n >= f_blocks, f_blocks, 0)
            pb = wide_proj(u, [w_in_t], i, [skip_f], [], [], (main_cols - d) // tn_in,
                           _identity_epilogue, BF16, "even_in_proj", tn=tn_in,
                           w_is_transposed=True)
            f_pre = wide_proj(u, [w_in_t], i, [plus(f_blocks)], [], [], f_blocks,
                              _identity_epilogue, F32, "even_f_proj", tn=tn_in,
                              w_is_transposed=True)
            dt_raw = small_proj(u, w_in_t, i, main_cols, n_ssd_heads)
            o_a = hgrn_mix(pb, f_pre, lower_bounds[i], hgrn_norm_g[i], batch, seq, d)
            o_b = ssd_mix(pb, dt_raw, ssd_conv_w[i], ssd_conv_b[i], ssd_dt_bias[i],
                          ssd_a_log[i], ssd_d[i], ssd_norm_g[i], batch, seq, d)
            h, u = out_proj([o_a, o_b], even_w_out, i, h, mix_post_g[layer],
                            ffn_pre_g[layer], True, "even_out_proj")
        else:
            c = wide_proj(u, [conf_w1, conf_w1], i, [plus(0), plus(d // tn)],
                          [conf_b1_3d, conf_b1_3d], [plus(0), plus(d // tn)], d // tn,
                          _glu_epilogue, F32, "conf_glu", tn=tn)
            h, u = conf_tail(c, conf_dw_w[i], conf_dw_b[i], conf_ln_g[i], conf_ln_b[i],
                             conf_w2, i, conf_b2[i], h, mix_post_g[layer],
                             ffn_pre_g[layer], batch, seq)
        act = wide_proj(u, [ffn_w_gate, ffn_w_up], layer, [plus(0), plus(0)], [], [],
                        hidden // tn, _swiglu_epilogue, BF16, "ffn_in", tn=tn)
        last = layer == depth - 1
        gnext = mix_pre_g[layer] if last else mix_pre_g[layer + 1]
        h, u = out_proj([act], ffn_w_down, layer, h, ffn_post_g[layer], gnext, not last,
                        "ffn_out")
    return h.reshape(batch, seq, d)
```

```python
import functools

import jax
import jax.numpy as jnp
from jax import lax
from jax.experimental import pallas as pl
from jax.experimental.pallas import tpu as pltpu

F32 = jnp.float32
BF16 = jnp.bfloat16

LOG2_E = 1.4426950408889634
RMS_EPS = 1e-6
LN_EPS = 1e-5
HGRN_F_MIN = 1e-6
HGRN_HEAD = 128
HGRN_CHUNK = 64
HGRN_SUB = 8
HGRN_HEADS_PER_PART = 4
HGRN_SEQS_PER_STEP = 2
HGRN_PARTS_PER_STEP = 4
SSD_HEAD_DIM = 64
SSD_GROUPS = 4
SSD_STATE = 128
SSD_CONV = 4
SSD_CHUNK = 128
SSD_GROUPS_PER_STEP = 4
CONF_KERNEL = 31
CONF_HALO = 32
WEIGHT_STAGE_ROWS = 512
LANES = 128
VMEM_LIMIT = 56 * 1024 * 1024


def _cparams(semantics):
    return pltpu.CompilerParams(dimension_semantics=semantics,
                                vmem_limit_bytes=VMEM_LIMIT)


def _sigmoid(x):
    return 0.5 * jnp.tanh(0.5 * x) + 0.5


def _silu(x):
    t = 0.5 * x
    return t * jnp.tanh(t) + t


def _softplus(x):
    return jnp.maximum(x, 0.0) + jnp.log1p(jnp.exp(-jnp.abs(x)))


def _rms_rows(x, g, eps=RMS_EPS):
    ms = jnp.mean(x * x, axis=-1, keepdims=True)
    return x * lax.rsqrt(ms + eps) * g


def _dot(a, b):
    return jnp.dot(a, b, preferred_element_type=F32)


def _dot_nt(a, b):
    return lax.dot_general(a, b, (((1,), (1,)), ((), ())), preferred_element_type=F32)


def _dot_tn(a, b):
    return lax.dot_general(a, b, (((0,), (0,)), ((), ())), preferred_element_type=F32)


def _split3(x):
    hi = x.astype(BF16)
    r1 = x - hi.astype(F32)
    mid = r1.astype(BF16)
    lo = (r1 - mid.astype(F32)).astype(BF16)
    return hi, mid, lo


def _dot_exact_rhs(sel, x):
    return _dot(jnp.concatenate([sel, sel, sel], axis=1),
                jnp.concatenate(_split3(x), axis=0))


def _lane_stack3(x, n):
    hi, mid, lo = _split3(x)
    return (hi.astype(F32) + pltpu.roll(mid.astype(F32), n, 1)
            + pltpu.roll(lo.astype(F32), 2 * n, 1)).astype(BF16)


def _lower_tri(n, dtype):
    r = lax.broadcasted_iota(jnp.int32, (n, n), 0)
    c = lax.broadcasted_iota(jnp.int32, (n, n), 1)
    return (r >= c).astype(dtype)


def _prenorm_kernel(x_ref, g_ref, o_ref):
    o_ref[...] = _rms_rows(x_ref[...], g_ref[...]).astype(o_ref.dtype)


def prenorm(x, g, tm=512):
    m, d = x.shape
    tm = min(tm, m)
    return pl.pallas_call(
        _prenorm_kernel,
        grid=(m // tm,),
        in_specs=[pl.BlockSpec((tm, d), lambda i: (i, 0)),
                  pl.BlockSpec((1, d), lambda i: (0, 0))],
        out_specs=pl.BlockSpec((tm, d), lambda i: (i, 0)),
        out_shape=jax.ShapeDtypeStruct((m, d), BF16),
        compiler_params=_cparams(("arbitrary",)),
        name="prenorm",
    )(x, g.reshape(1, d))


def _wide_kernel(*refs, n_w, n_b, epilogue, w_is_transposed):
    u_ref = refs[0]
    w_refs = refs[1:1 + n_w]
    b_refs = refs[1 + n_w:1 + n_w + n_b]
    o_ref = refs[1 + n_w + n_b]
    s_refs = refs[2 + n_w + n_b:]

    @pl.when(pl.program_id(1) == 0)
    def _():
        for w, s in zip(w_refs, s_refs):
            s[...] = w[...].astype(BF16)

    u = u_ref[...]
    dot = _dot_nt if w_is_transposed else _dot
    ys = [dot(u, s[...]) for s in s_refs]
    o_ref[...] = epilogue(*ys, *[b[...] for b in b_refs]).astype(o_ref.dtype)


def wide_proj(u, ws, layer, w_col_maps, bs, b_col_maps, n_blocks, epilogue, out_dtype,
              name, tn=512, tm=1024, w_is_transposed=False):
    m, k = u.shape
    tm = min(tm, m)
    in_specs = [pl.BlockSpec((tm, k), lambda n, i: (i, 0))]
    for cmap in w_col_maps:
        if w_is_transposed:
            in_specs.append(pl.BlockSpec((None, tn, k), lambda n, i, cmap=cmap: (layer, cmap(n), 0)))
        else:
            in_specs.append(pl.BlockSpec((None, k, tn), lambda n, i, cmap=cmap: (layer, 0, cmap(n))))
    for cmap in b_col_maps:
        in_specs.append(pl.BlockSpec((None, 1, tn), lambda n, i, cmap=cmap: (layer, 0, cmap(n))))
    return pl.pallas_call(
        functools.partial(_wide_kernel, n_w=len(ws), n_b=len(bs), epilogue=epilogue,
                          w_is_transposed=w_is_transposed),
        grid=(n_blocks, m // tm),
        in_specs=in_specs,
        out_specs=pl.BlockSpec((tm, tn), lambda n, i: (i, n)),
        out_shape=jax.ShapeDtypeStruct((m, n_blocks * tn), out_dtype),
        scratch_shapes=[pltpu.VMEM((tn, k) if w_is_transposed else (k, tn), BF16) for _ in ws],
        compiler_params=_cparams(("arbitrary", "arbitrary")),
        name=name,
    )(u, *ws, *bs)


def _identity_epilogue(y):
    return y


def _swiglu_epilogue(g, up):
    return _silu(g) * up


def _glu_epilogue(a, g, ba, bg):
    return (a + ba) * _sigmoid(g + bg)


def _small_proj_kernel(u_ref, w_ref, o_ref):
    n, k = w_ref.shape
    w = jnp.concatenate([w_ref[...], jnp.zeros((LANES - n, k), F32)], axis=0)
    o_ref[...] = _dot_nt(u_ref[...], w.astype(BF16))


def small_proj(u, w_t_stack, layer, first_row, n, tm=1024):
    m, k = u.shape
    tm = min(tm, m)
    return pl.pallas_call(
        _small_proj_kernel,
        grid=(m // tm,),
        in_specs=[pl.BlockSpec((tm, k), lambda i: (i, 0)),
                  pl.BlockSpec((None, n, k), lambda i: (layer, first_row // n, 0))],
        out_specs=pl.BlockSpec((tm, LANES), lambda i: (i, 0)),
        out_shape=jax.ShapeDtypeStruct((m, LANES), F32),
        compiler_params=_cparams(("arbitrary",)),
        name="dt_proj",
    )(u, w_t_stack)


def _residual_epilogue(m, h, gpost, gnext, hout_ref, unext_ref):
    hn = h + _rms_rows(m, gpost)
    hout_ref[...] = hn
    if unext_ref is not None:
        unext_ref[...] = _rms_rows(hn, gnext).astype(unext_ref.dtype)


def _load_weight_bf16(w_hbm, layer, w16_ref, stage_ref, sem_ref):
    rows = stage_ref.shape[1]
    n_chunks = w16_ref.shape[0] // rows

    def chunk_copy(c):
        return pltpu.make_async_copy(w_hbm.at[layer, pl.ds(c * rows, rows), :],
                                     stage_ref.at[c % 2], sem_ref.at[c % 2])

    chunk_copy(0).start()
    for c in range(n_chunks):
        if c + 1 < n_chunks:
            chunk_copy(c + 1).start()
        chunk_copy(c).wait()
        w16_ref[c * rows:(c + 1) * rows, :] = stage_ref[c % 2].astype(BF16)


def _out_proj_kernel(*refs, n_a, with_next, layer):
    a_refs = refs[:n_a]
    w_hbm, h_ref, gpost_ref, gnext_ref, hout_ref = refs[n_a:n_a + 5]
    unext_ref = refs[n_a + 5] if with_next else None
    w16_ref, stage_ref, sem_ref = refs[-3:]

    @pl.when(pl.program_id(0) == 0)
    def _():
        _load_weight_bf16(w_hbm, layer, w16_ref, stage_ref, sem_ref)

    a = jnp.concatenate([a_ref[...] for a_ref in a_refs], axis=1)
    m = _dot(a, w16_ref[...])
    _residual_epilogue(m, h_ref[...], gpost_ref[...], gnext_ref[...], hout_ref, unext_ref)


def out_proj(acts, w_stack, layer, h, gpost, gnext, with_next, name, tm=256,
             stage_rows=WEIGHT_STAGE_ROWS):
    m = h.shape[0]
    _, kk, d = w_stack.shape
    tm = min(tm, m)
    stage_rows = min(stage_rows, kk)
    row = pl.BlockSpec((tm, d), lambda i: (i, 0))
    vec = pl.BlockSpec((1, d), lambda i: (0, 0))
    out_shape = [jax.ShapeDtypeStruct((m, d), F32)]
    out_specs = [row]
    if with_next:
        out_shape.append(jax.ShapeDtypeStruct((m, d), BF16))
        out_specs.append(row)
    res = pl.pallas_call(
        functools.partial(_out_proj_kernel, n_a=len(acts), with_next=with_next, layer=layer),
        grid=(m // tm,),
        in_specs=[pl.BlockSpec((tm, a.shape[1]), lambda i: (i, 0)) for a in acts] + [
            pl.BlockSpec(memory_space=pl.ANY), row, vec, vec],
        out_specs=out_specs,
        out_shape=out_shape,
        scratch_shapes=[pltpu.VMEM((kk, d), BF16), pltpu.VMEM((2, stage_rows, d), F32),
                        pltpu.SemaphoreType.DMA((2,))],
        compiler_params=_cparams(("arbitrary",)),
        name=name,
    )(*acts, w_stack, h, gpost.reshape(1, d), gnext.reshape(1, d))
    return (res[0], res[1]) if with_next else (res[0], None)


def _hgrn_kernel(q_ref, f_ref, v_ref, gate_ref, lb_ref, an_ref, o_ref, st_ref,
                 *, n_chunks, n_heads, part_heads, n_seq):
    C, SUB, HD = HGRN_CHUNK, HGRN_SUB, HGRN_HEAD
    n_sub = C // SUB
    mid = SUB // 2 - 1
    width = part_heads * HD
    head_lanes = [slice(h * HD, (h + 1) * HD) for h in range(part_heads)]
    parts = [slice(p * width, (p + 1) * width) for p in range(n_heads // part_heads)]

    @pl.when(pl.program_id(2) == 0)
    def _():
        st_ref[...] = jnp.zeros_like(st_ref)

    tri = _lower_tri(C, BF16)
    causal = (lax.broadcasted_iota(jnp.int32, (C, C), 0)
              >= lax.broadcasted_iota(jnp.int32, (C, C), 1))
    def gates(sl, bi, part):
        lb = lb_ref[:, part]
        f = (0.5 + 0.5 * lb) + (0.5 - 0.5 * lb) * jnp.tanh(0.5 * f_ref[bi, sl, part])
        k = 1.0 - f
        log2_f = jnp.log(jnp.maximum(f, HGRN_F_MIN)) * LOG2_E
        qs = _silu(q_ref[bi, sl, part].astype(F32))
        b = _dot_exact_rhs(tri, log2_f)
        return qs, k, b, v_ref[bi, sl, part]

    def scores_and_state(vals, h0):
        qs, k, b, v16 = vals
        b_last = b[C - 1:C, :]

        q_in16 = (qs * jnp.exp2(b)).astype(BF16)
        k_end16 = (k * jnp.exp2(b_last - b)).astype(BF16)
        dec_last = jnp.exp2(b_last)
        sts = [st_ref[h0 + h] for h in range(part_heads)]
        o_heads = [_dot(q_in16[:, s], st.astype(BF16)) for s, st in zip(head_lanes, sts)]
        for h, (s, st) in enumerate(zip(head_lanes, sts)):
            row_decay = jnp.broadcast_to(dec_last[:, s], (HD, HD)).T
            st_ref[h0 + h] = st * row_decay + _dot_tn(k_end16[:, s], v16[:, s])

        blocks = [[] for _ in range(part_heads)]
        keys = []
        prev_ref = None
        for i in range(n_sub):
            rows = slice(i * SUB, (i + 1) * SUB)
            ref_b = b[i * SUB + mid:i * SUB + mid + 1, :]
            if keys:
                step = jnp.exp2(ref_b - prev_ref)
                keys = [kj * step for kj in keys]
            keys.append(k[rows, :] * jnp.exp2(ref_b - b[rows, :]))
            prev_ref = ref_b
            qt16 = (qs[rows, :] * jnp.exp2(b[rows, :] - ref_b)).astype(BF16)
            pad = [jnp.zeros(((n_sub - 1 - i) * SUB, width), F32)] if i < n_sub - 1 else []
            kt16 = jnp.concatenate(keys + pad, axis=0).astype(BF16)
            for h, s in enumerate(head_lanes):
                blocks[h].append(_dot_nt(qt16[:, s], kt16[:, s]))
        return o_heads, blocks

    def outputs(sl, bi, part, vals, o_heads, blocks):
        v16 = vals[3]
        for h, s in enumerate(head_lanes):
            scores = jnp.where(causal, jnp.concatenate(blocks[h], axis=0), 0.0)
            o_heads[h] = o_heads[h] + _dot(scores.astype(BF16), v16[:, s])
        y = jnp.concatenate(
            [o * lax.rsqrt(jnp.mean(o * o, axis=-1, keepdims=True) + RMS_EPS) for o in o_heads],
            axis=1)
        y = y * an_ref[:, part] * _silu(gate_ref[bi, sl, part].astype(F32))
        o_ref[bi, sl, part] = y.astype(o_ref.dtype)

    def chunk(c, carry):
        sl = pl.ds(pl.multiple_of(c * C, C), C)
        units = [(bi, part) for bi in range(n_seq) for part in parts]
        vals, mids = {}, {}
        for p in range(len(units) + 2):
            if p < len(units):
                vals[p] = gates(sl, *units[p])
            if 1 <= p <= len(units):
                mids[p - 1] = scores_and_state(vals[p - 1], (p - 1) * part_heads)
            if p >= 2:
                outputs(sl, *units[p - 2], vals[p - 2], *mids[p - 2])
        return carry

    lax.fori_loop(0, n_chunks, chunk, 0)


def hgrn_mix(pb, f_pre, lb, a_norm, batch, seq, d_model, rows_per_step=256):
    n_heads = d_model // HGRN_HEAD
    part_heads = min(HGRN_HEADS_PER_PART, n_heads)
    hb = min(HGRN_PARTS_PER_STEP * part_heads, n_heads)
    gw = hb * HGRN_HEAD
    n_groups = n_heads // hb
    n_seq = min(HGRN_SEQS_PER_STEP, batch)
    cs = min(rows_per_step, seq)
    m = batch * seq

    def col(seg):
        return pl.BlockSpec((n_seq, cs, gw), lambda b, h, s, seg=seg: (b, s, seg * n_groups + h))

    vec = pl.BlockSpec((1, gw), lambda b, h, s: (0, h))
    pb3 = pb.reshape(batch, seq, -1)
    out = pl.pallas_call(
        functools.partial(_hgrn_kernel, n_chunks=cs // HGRN_CHUNK, n_heads=hb,
                          part_heads=part_heads, n_seq=n_seq),
        grid=(batch // n_seq, n_groups, seq // cs),
        in_specs=[col(0), col(0), col(1), col(2), vec, vec],
        out_specs=col(0),
        out_shape=jax.ShapeDtypeStruct((batch, seq, d_model), BF16),
        scratch_shapes=[pltpu.VMEM((n_seq * hb, HGRN_HEAD, HGRN_HEAD), F32)],
        compiler_params=_cparams(("arbitrary", "arbitrary", "arbitrary")),
        name="hgrn2",
    )(pb3, f_pre.reshape(batch, seq, d_model), pb3, pb3,
      lb.reshape(1, d_model), a_norm.reshape(1, d_model))
    return out.reshape(m, d_model)


def _causal_conv(raw_ref, carry_ref, ext_ref, w_ref, b_ref, n_rows, n_taps, halo):
    first = halo - n_taps + 1
    outs = []
    for j in range(raw_ref.shape[1] // LANES):
        ln = slice(j * LANES, (j + 1) * LANES)
        raw = raw_ref[:, ln].astype(F32)
        ext_ref[j, 0:halo, :] = carry_ref[j]
        ext_ref[j, halo:halo + n_rows, :] = raw
        carry_ref[j] = raw[n_rows - halo:n_rows, :]
        acc = b_ref[:, ln] + w_ref[0:1, ln] * ext_ref[j, pl.ds(first, n_rows), :]
        for k in range(1, n_taps):
            acc = acc + w_ref[k:k + 1, ln] * ext_ref[j, pl.ds(first + k, n_rows), :]
        outs.append(acc)
    return jnp.concatenate(outs, axis=1)


def _ssd_kernel(x_ref, bm_ref, cm_ref, z_ref, dt_ref,
                cwx_ref, cwb_ref, cwc_ref, cbx_ref, cbb_ref, cbc_ref,
                dtb_ref, alog_ref, selp_ref, selw_ref, dskip_ref, norm_ref, o_ref,
                st_ref, carx_ref, carb_ref, carc_ref, extx_ref, extb_ref, extc_ref,
                *, heads_per_group, n_groups, n_heads):
    L, P, R, N = SSD_CHUNK, SSD_HEAD_DIM, heads_per_group, SSD_STATE
    gw = R * P
    groups = range(n_groups)
    g_lanes = [slice(j * gw, (j + 1) * gw) for j in groups]
    n_lanes = [slice(j * N, (j + 1) * N) for j in groups]

    @pl.when(pl.program_id(2) == 0)
    def _():
        st_ref[...] = jnp.zeros_like(st_ref)
        carx_ref[...] = jnp.zeros_like(carx_ref)
        carb_ref[...] = jnp.zeros_like(carb_ref)
        carc_ref[...] = jnp.zeros_like(carc_ref)

    halo = carx_ref.shape[1]
    xc = _silu(_causal_conv(x_ref, carx_ref, extx_ref, cwx_ref, cbx_ref, L, SSD_CONV, halo))
    bc16 = _silu(_causal_conv(bm_ref, carb_ref, extb_ref, cwb_ref, cbb_ref, L, SSD_CONV,
                              halo)).astype(BF16)
    cc16 = _silu(_causal_conv(cm_ref, carc_ref, extc_ref, cwc_ref, cbc_ref, L, SSD_CONV,
                              halo)).astype(BF16)

    lane = lax.broadcasted_iota(jnp.int32, (L, LANES), 1)
    dt = jnp.where(lane < n_heads, _softplus(dt_ref[...] + dtb_ref[...]), 0.0)
    dta = dt * (-LOG2_E * jnp.exp(alog_ref[...]))
    cs = _dot_exact_rhs(_lower_tri(L, BF16), dta)

    dt3 = _lane_stack3(dt, n_heads)
    cs3 = _lane_stack3(cs, n_heads)
    dt_x = [_dot(dt3, selp_ref[j]) for j in groups]
    cs_x = [_dot(cs3, selp_ref[j]) for j in groups]
    cs_w = [_dot(cs3, selw_ref[j]) for j in groups]

    rr = lax.broadcasted_iota(jnp.int32, (L, L), 0)
    cl = lax.broadcasted_iota(jnp.int32, (L, L), 1)
    causal = rr >= cl
    ys, xdt16s, cbs = [], [], []
    for j in groups:
        xdt = xc[:, g_lanes[j]] * dt_x[j]
        cs_last = cs_x[j][L - 1:L, :]
        st = st_ref[j]
        ys.append(_dot(cc16[:, n_lanes[j]], st.astype(BF16)) * jnp.exp2(cs_x[j]))
        st_ref[j] = st * jnp.exp2(cs_last) + _dot_tn(
            bc16[:, n_lanes[j]], (xdt * jnp.exp2(cs_last - cs_x[j])).astype(BF16))
        cb = _dot_nt(cc16[:, n_lanes[j]], bc16[:, n_lanes[j]])
        cbs.append(jnp.where(causal, cb, 0.0))
        xdt16s.append(xdt.astype(BF16))

    parts = [[] for _ in groups]
    for r in range(R):
        for j in groups:
            col = cs_w[j][:, r * L:(r + 1) * L]
            seg = col - col.T
            decay = jnp.exp2(jnp.minimum(seg, 0.0))
            parts[j].append(_dot((cbs[j] * decay).astype(BF16),
                                 xdt16s[j][:, r * P:(r + 1) * P]))

    for j in groups:
        ln = g_lanes[j]
        y = ys[j] + jnp.concatenate(parts[j], axis=1) + dskip_ref[:, ln] * xc[:, ln]
        yz = y * _silu(z_ref[:, ln].astype(F32))
        o_ref[:, ln] = _rms_rows(yz, norm_ref[:, ln]).astype(o_ref.dtype)


def ssd_mix(pb, dt_raw, conv_w, conv_b, dt_bias, a_log, d_skip, b_norm,
            batch, seq, d_model):
    L, G, N = SSD_CHUNK, SSD_GROUPS, SSD_STATE
    n_heads = d_model // SSD_HEAD_DIM
    R = n_heads // G
    gw = d_model // G
    steps = seq // L
    m = batch * seq
    halo = 8
    gs = SSD_GROUPS_PER_STEP
    gsw, gsn = gs * gw, gs * N
    bn0 = 5 * d_model // gsn
    cwb0 = d_model // gsn

    def rep_heads(v):
        return jnp.repeat(v.astype(F32), SSD_HEAD_DIM).reshape(1, d_model)

    def pad_heads(v):
        return jnp.pad(v.astype(F32), (0, LANES - n_heads)).reshape(1, LANES)

    def select(width):
        row = jnp.arange(LANES, dtype=jnp.int32)[None, :, None]
        lane = jnp.arange(R * width, dtype=jnp.int32)[None, None, :]
        grp = jnp.arange(G, dtype=jnp.int32)[:, None, None]
        hit = (row % n_heads == grp * R + lane // width) & (row < 3 * n_heads)
        return hit.astype(BF16)

    row_g = lambda off: pl.BlockSpec((L, gsw), lambda b, g, s, off=off: (b * steps + s, off + g))
    row_n = lambda off: pl.BlockSpec((L, gsn), lambda b, g, s, off=off: (b * steps + s, off + g))
    par_g = lambda rows: pl.BlockSpec((rows, gsw), lambda b, g, s: (0, g))
    par_n = lambda rows, off: pl.BlockSpec((rows, gsn), lambda b, g, s, off=off: (0, off + g))
    head_vec = pl.BlockSpec((1, LANES), lambda b, g, s: (0, 0))
    sel = lambda width: pl.BlockSpec((gs, LANES, R * width), lambda b, g, s: (g, 0, 0))
    cb2 = conv_b.reshape(1, -1)
    n_steps_g = G // gs

    return pl.pallas_call(
        functools.partial(_ssd_kernel, heads_per_group=R, n_groups=gs, n_heads=n_heads),
        grid=(batch, n_steps_g, steps),
        in_specs=[row_g(4 * n_steps_g), row_n(bn0), row_n(bn0 + n_steps_g), row_g(3 * n_steps_g),
                  pl.BlockSpec((L, LANES), lambda b, g, s: (b * steps + s, 0)),
                  par_g(SSD_CONV), par_n(SSD_CONV, cwb0), par_n(SSD_CONV, cwb0 + n_steps_g),
                  par_g(1), par_n(1, cwb0), par_n(1, cwb0 + n_steps_g),
                  head_vec, head_vec, sel(SSD_HEAD_DIM), sel(L), par_g(1), par_g(1)],
        out_specs=pl.BlockSpec((L, gsw), lambda b, g, s: (b * steps + s, g)),
        out_shape=jax.ShapeDtypeStruct((m, d_model), BF16),
        scratch_shapes=[pltpu.VMEM((gs, N, gw), F32),
                        pltpu.VMEM((gsw // LANES, halo, LANES), F32),
                        pltpu.VMEM((gsn // LANES, halo, LANES), F32),
                        pltpu.VMEM((gsn // LANES, halo, LANES), F32),
                        pltpu.VMEM((gsw // LANES, halo + L, LANES), F32),
                        pltpu.VMEM((gsn // LANES, halo + L, LANES), F32),
                        pltpu.VMEM((gsn // LANES, halo + L, LANES), F32)],
        compiler_params=_cparams(("arbitrary", "arbitrary", "arbitrary")),
        name="ssd",
    )(pb, pb, pb, pb, dt_raw,
      conv_w, conv_w, conv_w, cb2, cb2, cb2,
      pad_heads(dt_bias), pad_heads(a_log),
      select(SSD_HEAD_DIM), select(L), rep_heads(d_skip), b_norm.reshape(1, d_model))


def _conf_tail_kernel(c_ref, dww_ref, dwb_ref, lng_ref, lnb_ref, w2_hbm, b2_ref,
                      h_ref, gpost_ref, gnext_ref, hout_ref, unext_ref,
                      carry_ref, ext_ref, conv_ref, w2_ref, stage_ref, sem_ref,
                      *, row_block, layer):
    tm, d = c_ref.shape
    halo = CONF_HALO
    first = halo - (CONF_KERNEL - 1)
    n_tiles = d // LANES

    @pl.when((pl.program_id(0) == 0) & (pl.program_id(1) == 0))
    def _():
        _load_weight_bf16(w2_hbm, layer, w2_ref, stage_ref, sem_ref)

    @pl.when(pl.program_id(1) == 0)
    def _():
        carry_ref[...] = jnp.zeros_like(carry_ref)

    for j in range(n_tiles):
        ln = slice(j * LANES, (j + 1) * LANES)
        ext_ref[j, 0:halo, :] = carry_ref[j]
        ext_ref[j, halo:halo + tm, :] = c_ref[:, ln]
        carry_ref[j] = c_ref[tm - halo:tm, ln]

    def col_tile(j, carry):
        cj = pl.ds(pl.multiple_of(j * LANES, LANES), LANES)
        for rb in range(tm // row_block):
            base = rb * row_block
            acc = dwb_ref[:, cj] + dww_ref[0:1, cj] * ext_ref[j, pl.ds(base + first, row_block), :]
            for k in range(1, CONF_KERNEL):
                acc = acc + dww_ref[k:k + 1, cj] * ext_ref[j, pl.ds(base + first + k, row_block), :]
            conv_ref[base:base + row_block, cj] = acc
        return carry

    lax.fori_loop(0, n_tiles, col_tile, 0)

    x = conv_ref[...]
    mu = jnp.mean(x, axis=-1, keepdims=True)
    xc = x - mu
    var = jnp.mean(xc * xc, axis=-1, keepdims=True)
    y = _silu(xc * lax.rsqrt(var + LN_EPS) * lng_ref[...] + lnb_ref[...])
    m = _dot(y.astype(BF16), w2_ref[...]) + b2_ref[...]
    _residual_epilogue(m, h_ref[...], gpost_ref[...], gnext_ref[...], hout_ref, unext_ref)


def conf_tail(c, dw_w, dw_b, ln_g, ln_b, w2_stack, layer, b2, h, gpost, gnext,
              batch, seq, tm=512, row_block=64, stage_rows=WEIGHT_STAGE_ROWS // 2):
    m, d = c.shape
    tm = min(tm, seq)
    row_block = min(row_block, tm)
    stage_rows = min(stage_rows, d)
    steps = seq // tm
    row = pl.BlockSpec((tm, d), lambda b, s: (b * steps + s, 0))
    vec = pl.BlockSpec((1, d), lambda b, s: (0, 0))
    dww = jnp.pad(dw_w, ((0, CONF_HALO - CONF_KERNEL), (0, 0)))
    return pl.pallas_call(
        functools.partial(_conf_tail_kernel, row_block=row_block, layer=layer),
        grid=(batch, steps),
        in_specs=[row, pl.BlockSpec((CONF_HALO, d), lambda b, s: (0, 0)), vec, vec, vec,
                  pl.BlockSpec(memory_space=pl.ANY),
                  vec, row, vec, vec],
        out_specs=[row, row],
        out_shape=[jax.ShapeDtypeStruct((m, d), F32), jax.ShapeDtypeStruct((m, d), BF16)],
        scratch_shapes=[pltpu.VMEM((d // LANES, CONF_HALO, LANES), F32),
                        pltpu.VMEM((d // LANES, CONF_HALO + tm, LANES), F32),
                        pltpu.VMEM((tm, d), F32),
                        pltpu.VMEM((d, d), BF16), pltpu.VMEM((2, stage_rows, d), F32),
                        pltpu.SemaphoreType.DMA((2,))],
        compiler_params=_cparams(("arbitrary", "arbitrary")),
        name="conf_tail",
    )(c, dww, dw_b.reshape(1, d), ln_g.reshape(1, d), ln_b.reshape(1, d), w2_stack,
      b2.reshape(1, d), h, gpost.reshape(1, d), gnext.reshape(1, d))


def kernel(x, mix_pre_g, mix_post_g, ffn_pre_g, ffn_post_g, hgrn_lb_logits, even_w_in,
           hgrn_norm_g, ssd_conv_w, ssd_conv_b, ssd_dt_bias, ssd_a_log, ssd_d, ssd_norm_g,
           even_w_out, conf_w1, conf_b1, conf_dw_w, conf_dw_b, conf_ln_g, conf_ln_b,
           conf_w2, conf_b2, ffn_w_gate, ffn_w_up, ffn_w_down):
    batch, seq, d = x.shape
    depth = mix_pre_g.shape[0]
    hidden = ffn_w_gate.shape[2]
    m = batch * seq
    tn = 512
    tn_in = min(1024, d)
    main_cols = 6 * d + 2 * SSD_GROUPS * SSD_STATE
    n_ssd_heads = d // SSD_HEAD_DIM
    f_blocks = d // tn_in

    lb_p = jax.nn.softmax(hgrn_lb_logits.astype(F32), axis=0)
    lower_bounds = jnp.cumsum(lb_p, axis=0) - lb_p[0]

    w_in_t = jnp.swapaxes(even_w_in, 1, 2)
    conf_b1_3d = conf_b1.reshape(conf_b1.shape[0], 1, -1)
    plus = lambda off: (lambda n: n + off)

    h = x.reshape(m, d)
    u = prenorm(h, mix_pre_g[0])
    for layer in range(depth):
        i = layer // 2
        if layer % 2 == 0:
            skip_f = lambda n: n + jnp.where(n >= f_blocks, f_blocks, 0)
            pb = wide_proj(u, [w_in_t], i, [skip_f], [], [], (main_cols - d) // tn_in,
                           _identity_epilogue, BF16, "even_in_proj", tn=tn_in,
                           w_is_transposed=True)
            f_pre = wide_proj(u, [w_in_t], i, [plus(f_blocks)], [], [], f_blocks,
                              _identity_epilogue, F32, "even_f_proj", tn=tn_in,
                              w_is_transposed=True)
            dt_raw = small_proj(u, w_in_t, i, main_cols, n_ssd_heads)
            o_a = hgrn_mix(pb, f_pre, lower_bounds[i], hgrn_norm_g[i], batch, seq, d)
            o_b = ssd_mix(pb, dt_raw, ssd_conv_w[i], ssd_conv_b[i], ssd_dt_bias[i],
                          ssd_a_log[i], ssd_d[i], ssd_norm_g[i], batch, seq, d)
            h, u = out_proj([o_a, o_b], even_w_out, i, h, mix_post_g[layer],
                            ffn_pre_g[layer], True, "even_out_proj")
        else:
            c = wide_proj(u, [conf_w1, conf_w1], i, [plus(0), plus(d // tn)],
                          [conf_b1_3d, conf_b1_3d], [plus(0), plus(d // tn)], d // tn,
                          _glu_epilogue, F32, "conf_glu", tn=tn)
            h, u = conf_tail(c, conf_dw_w[i], conf_dw_b[i], conf_ln_g[i], conf_ln_b[i],
                             conf_w2, i, conf_b2[i], h, mix_post_g[layer],
                             ffn_pre_g[layer], batch, seq)
        act = wide_proj(u, [ffn_w_gate, ffn_w_up], layer, [plus(0), plus(0)], [], [],
                        hidden // tn, _swiglu_epilogue, BF16, "ffn_in", tn=tn)
        last = layer == depth - 1
        gnext = mix_pre_g[layer] if last else mix_pre_g[layer + 1]
        h, u = out_proj([act], ffn_w_down, layer, h, ffn_post_g[layer], gnext, not last,
                        "ffn_out")
    return h.reshape(batch, seq, d)
```

```python
import functools

import jax
import jax.numpy as jnp
from jax import lax
from jax.experimental import pallas as pl
from jax.experimental.pallas import tpu as pltpu

F32 = jnp.float32
BF16 = jnp.bfloat16

LOG2_E = 1.4426950408889634
RMS_EPS = 1e-6
LN_EPS = 1e-5
HGRN_F_MIN = 1e-6
HGRN_HEAD = 128
HGRN_CHUNK = 64
HGRN_SUB = 8
HGRN_HEADS_PER_PART = 4
HGRN_SEQS_PER_STEP = 2
HGRN_PARTS_PER_STEP = 4
SSD_HEAD_DIM = 64
SSD_GROUPS = 4
SSD_STATE = 128
SSD_CONV = 4
SSD_CHUNK = 128
SSD_GROUPS_PER_STEP = 4
CONF_KERNEL = 31
CONF_HALO = 32
WEIGHT_STAGE_ROWS = 512
LANES = 128
VMEM_LIMIT = 56 * 1024 * 1024


def _cparams(semantics):
    return pltpu.CompilerParams(dimension_semantics=semantics,
                                vmem_limit_bytes=VMEM_LIMIT)


def _sigmoid(x):
    return 0.5 * jnp.tanh(0.5 * x) + 0.5


def _silu(x):
    t = 0.5 * x
    return t * jnp.tanh(t) + t


def _softplus(x):
    return jnp.maximum(x, 0.0) + jnp.log1p(jnp.exp(-jnp.abs(x)))


def _rms_rows(x, g, eps=RMS_EPS):
    ms = jnp.mean(x * x, axis=-1, keepdims=True)
    return x * lax.rsqrt(ms + eps) * g


def _dot(a, b):
    return jnp.dot(a, b, preferred_element_type=F32)


def _dot_nt(a, b):
    return lax.dot_general(a, b, (((1,), (1,)), ((), ())), preferred_element_type=F32)


def _dot_tn(a, b):
    return lax.dot_general(a, b, (((0,), (0,)), ((), ())), preferred_element_type=F32)


def _split3(x):
    hi = x.astype(BF16)
    r1 = x - hi.astype(F32)
    mid = r1.astype(BF16)
    lo = (r1 - mid.astype(F32)).astype(BF16)
    return hi, mid, lo


def _dot_exact_rhs(sel, x):
    return _dot(jnp.concatenate([sel, sel, sel], axis=1),
                jnp.concatenate(_split3(x), axis=0))


def _lane_stack3(x, n):
    hi, mid, lo = _split3(x)
    return (hi.astype(F32) + pltpu.roll(mid.astype(F32), n, 1)
            + pltpu.roll(lo.astype(F32), 2 * n, 1)).astype(BF16)


def _lower_tri(n, dtype):
    r = lax.broadcasted_iota(jnp.int32, (n, n), 0)
    c = lax.broadcasted_iota(jnp.int32, (n, n), 1)
    return (r >= c).astype(dtype)


def _prenorm_kernel(x_ref, g_ref, o_ref):
    o_ref[...] = _rms_rows(x_ref[...], g_ref[...]).astype(o_ref.dtype)


def prenorm(x, g, tm=512):
    m, d = x.shape
    tm = min(tm, m)
    return pl.pallas_call(
        _prenorm_kernel,
        grid=(m // tm,),
        in_specs=[pl.BlockSpec((tm, d), lambda i: (i, 0)),
                  pl.BlockSpec((1, d), lambda i: (0, 0))],
        out_specs=pl.BlockSpec((tm, d), lambda i: (i, 0)),
        out_shape=jax.ShapeDtypeStruct((m, d), BF16),
        compiler_params=_cparams(("arbitrary",)),
        name="prenorm",
    )(x, g.reshape(1, d))


def _wide_kernel(*refs, n_w, n_b, epilogue, w_is_transposed):
    u_ref = refs[0]
    w_refs = refs[1:1 + n_w]
    b_refs = refs[1 + n_w:1 + n_w + n_b]
    o_ref = refs[1 + n_w + n_b]
    s_refs = refs[2 + n_w + n_b:]

    @pl.when(pl.program_id(1) == 0)
    def _():
        for w, s in zip(w_refs, s_refs):
            s[...] = w[...].astype(BF16)

    u = u_ref[...]
    dot = _dot_nt if w_is_transposed else _dot
    ys = [dot(u, s[...]) for s in s_refs]
    o_ref[...] = epilogue(*ys, *[b[...] for b in b_refs]).astype(o_ref.dtype)


def wide_proj(u, ws, layer, w_col_maps, bs, b_col_maps, n_blocks, epilogue, out_dtype,
              name, tn=512, tm=1024, w_is_transposed=False):
    m, k = u.shape
    tm = min(tm, m)
    in_specs = [pl.BlockSpec((tm, k), lambda n, i: (i, 0))]
    for cmap in w_col_maps:
        if w_is_transposed:
            in_specs.append(pl.BlockSpec((None, tn, k), lambda n, i, cmap=cmap: (layer, cmap(n), 0)))
        else:
            in_specs.append(pl.BlockSpec((None, k, tn), lambda n, i, cmap=cmap: (layer, 0, cmap(n))))
    for cmap in b_col_maps:
        in_specs.append(pl.BlockSpec((None, 1, tn), lambda n, i, cmap=cmap: (layer, 0, cmap(n))))
    return pl.pallas_call(
        functools.partial(_wide_kernel, n_w=len(ws), n_b=len(bs), epilogue=epilogue,
                          w_is_transposed=w_is_transposed),
        grid=(n_blocks, m // tm),
        in_specs=in_specs,
        out_specs=pl.BlockSpec((tm, tn), lambda n, i: (i, n)),
        out_shape=jax.ShapeDtypeStruct((m, n_blocks * tn), out_dtype),
        scratch_shapes=[pltpu.VMEM((tn, k) if w_is_transposed else (k, tn), BF16) for _ in ws],
        compiler_params=_cparams(("arbitrary", "arbitrary")),
        name=name,
    )(u, *ws, *bs)


def _identity_epilogue(y):
    return y


def _swiglu_epilogue(g, up):
    return _silu(g) * up


def _glu_epilogue(a, g, ba, bg):
    return (a + ba) * _sigmoid(g + bg)


def _residual_epilogue(m, h, gpost, gnext, hout_ref, unext_ref):
    hn = h + _rms_rows(m, gpost)
    hout_ref[...] = hn
    if unext_ref is not None:
        unext_ref[...] = _rms_rows(hn, gnext).astype(unext_ref.dtype)


def _load_weight_bf16(w_hbm, layer, w16_ref, stage_ref, sem_ref):
    rows = stage_ref.shape[1]
    n_chunks = w16_ref.shape[0] // rows

    def chunk_copy(c):
        return pltpu.make_async_copy(w_hbm.at[layer, pl.ds(c * rows, rows), :],
                                     stage_ref.at[c % 2], sem_ref.at[c % 2])

    chunk_copy(0).start()
    for c in range(n_chunks):
        if c + 1 < n_chunks:
            chunk_copy(c + 1).start()
        chunk_copy(c).wait()
        w16_ref[c * rows:(c + 1) * rows, :] = stage_ref[c % 2].astype(BF16)


def _out_proj_kernel(*refs, n_a, with_next, layer):
    a_refs = refs[:n_a]
    w_hbm, h_ref, gpost_ref, gnext_ref, hout_ref = refs[n_a:n_a + 5]
    unext_ref = refs[n_a + 5] if with_next else None
    w16_ref, stage_ref, sem_ref = refs[-3:]

    @pl.when(pl.program_id(0) == 0)
    def _():
        _load_weight_bf16(w_hbm, layer, w16_ref, stage_ref, sem_ref)

    a = jnp.concatenate([a_ref[...] for a_ref in a_refs], axis=1)
    m = _dot(a, w16_ref[...])
    _residual_epilogue(m, h_ref[...], gpost_ref[...], gnext_ref[...], hout_ref, unext_ref)


def out_proj(acts, w_stack, layer, h, gpost, gnext, with_next, name, tm=256,
             stage_rows=WEIGHT_STAGE_ROWS):
    m = h.shape[0]
    _, kk, d = w_stack.shape
    tm = min(tm, m)
    stage_rows = min(stage_rows, kk)
    row = pl.BlockSpec((tm, d), lambda i: (i, 0))
    vec = pl.BlockSpec((1, d), lambda i: (0, 0))
    out_shape = [jax.ShapeDtypeStruct((m, d), F32)]
    out_specs = [row]
    if with_next:
        out_shape.append(jax.ShapeDtypeStruct((m, d), BF16))
        out_specs.append(row)
    res = pl.pallas_call(
        functools.partial(_out_proj_kernel, n_a=len(acts), with_next=with_next, layer=layer),
        grid=(m // tm,),
        in_specs=[pl.BlockSpec((tm, a.shape[1]), lambda i: (i, 0)) for a in acts] + [
            pl.BlockSpec(memory_space=pl.ANY), row, vec, vec],
        out_specs=out_specs,
        out_shape=out_shape,
        scratch_shapes=[pltpu.VMEM((kk, d), BF16), pltpu.VMEM((2, stage_rows, d), F32),
                        pltpu.SemaphoreType.DMA((2,))],
        compiler_params=_cparams(("arbitrary",)),
        name=name,
    )(*acts, w_stack, h, gpost.reshape(1, d), gnext.reshape(1, d))
    return (res[0], res[1]) if with_next else (res[0], None)


def _hgrn_kernel(q_ref, f_ref, v_ref, gate_ref, lb_ref, an_ref, o_ref, st_ref,
                 *, n_chunks, n_heads, part_heads, n_seq):
    C, SUB, HD = HGRN_CHUNK, HGRN_SUB, HGRN_HEAD
    n_sub = C // SUB
    mid = SUB // 2 - 1
    width = part_heads * HD
    head_lanes = [slice(h * HD, (h + 1) * HD) for h in range(part_heads)]
    parts = [slice(p * width, (p + 1) * width) for p in range(n_heads // part_heads)]

    @pl.when(pl.program_id(2) == 0)
    def _():
        st_ref[...] = jnp.zeros_like(st_ref)

    tri = _lower_tri(C, BF16)
    causal = (lax.broadcasted_iota(jnp.int32, (C, C), 0)
              >= lax.broadcasted_iota(jnp.int32, (C, C), 1))
    def gates(sl, bi, part):
        lb = lb_ref[:, part]
        f = (0.5 + 0.5 * lb) + (0.5 - 0.5 * lb) * jnp.tanh(0.5 * f_ref[bi, sl, part])
        k = 1.0 - f
        log2_f = jnp.log(jnp.maximum(f, HGRN_F_MIN)) * LOG2_E
        qs = _silu(q_ref[bi, sl, part].astype(F32))
        b = _dot_exact_rhs(tri, log2_f)
        return qs, k, b, v_ref[bi, sl, part]

    def scores_and_state(vals, h0):
        qs, k, b, v16 = vals
        b_last = b[C - 1:C, :]

        q_in16 = (qs * jnp.exp2(b)).astype(BF16)
        k_end16 = (k * jnp.exp2(b_last - b)).astype(BF16)
        dec_last = jnp.exp2(b_last)
        sts = [st_ref[h0 + h] for h in range(part_heads)]
        o_heads = [_dot(q_in16[:, s], st.astype(BF16)) for s, st in zip(head_lanes, sts)]
        for h, (s, st) in enumerate(zip(head_lanes, sts)):
            row_decay = jnp.broadcast_to(dec_last[:, s], (HD, HD)).T
            st_ref[h0 + h] = st * row_decay + _dot_tn(k_end16[:, s], v16[:, s])

        blocks = [[] for _ in range(part_heads)]
        keys = []
        prev_ref = None
        for i in range(n_sub):
            rows = slice(i * SUB, (i + 1) * SUB)
            ref_b = b[i * SUB + mid:i * SUB + mid + 1, :]
            if keys:
                step = jnp.exp2(ref_b - prev_ref)
                keys = [kj * step for kj in keys]
            keys.append(k[rows, :] * jnp.exp2(ref_b - b[rows, :]))
            prev_ref = ref_b
            qt16 = (qs[rows, :] * jnp.exp2(b[rows, :] - ref_b)).astype(BF16)
            pad = [jnp.zeros(((n_sub - 1 - i) * SUB, width), F32)] if i < n_sub - 1 else []
            kt16 = jnp.concatenate(keys + pad, axis=0).astype(BF16)
            for h, s in enumerate(head_lanes):
                blocks[h].append(_dot_nt(qt16[:, s], kt16[:, s]))
        return o_heads, blocks

    def outputs(sl, bi, part, vals, o_heads, blocks):
        v16 = vals[3]
        for h, s in enumerate(head_lanes):
            scores = jnp.where(causal, jnp.concatenate(blocks[h], axis=0), 0.0)
            o_heads[h] = o_heads[h] + _dot(scores.astype(BF16), v16[:, s])
        y = jnp.concatenate(
            [o * lax.rsqrt(jnp.mean(o * o, axis=-1, keepdims=True) + RMS_EPS) for o in o_heads],
            axis=1)
        y = y * an_ref[:, part] * _silu(gate_ref[bi, sl, part].astype(F32))
        o_ref[bi, sl, part] = y.astype(o_ref.dtype)

    def chunk(c, carry):
        sl = pl.ds(pl.multiple_of(c * C, C), C)
        units = [(bi, part) for bi in range(n_seq) for part in parts]
        vals, mids = {}, {}
        for p in range(len(units) + 2):
            if p < len(units):
                vals[p] = gates(sl, *units[p])
            if 1 <= p <= len(units):
                mids[p - 1] = scores_and_state(vals[p - 1], (p - 1) * part_heads)
            if p >= 2:
                outputs(sl, *units[p - 2], vals[p - 2], *mids[p - 2])
        return carry

    lax.fori_loop(0, n_chunks, chunk, 0)


def hgrn_mix(pb, f_pre, lb, a_norm, batch, seq, d_model, rows_per_step=256):
    n_heads = d_model // HGRN_HEAD
    part_heads = min(HGRN_HEADS_PER_PART, n_heads)
    hb = min(HGRN_PARTS_PER_STEP * part_heads, n_heads)
    gw = hb * HGRN_HEAD
    n_groups = n_heads // hb
    n_seq = min(HGRN_SEQS_PER_STEP, batch)
    cs = min(rows_per_step, seq)
    m = batch * seq

    def col(seg):
        return pl.BlockSpec((n_seq, cs, gw), lambda b, h, s, seg=seg: (b, s, seg * n_groups + h))

    vec = pl.BlockSpec((1, gw), lambda b, h, s: (0, h))
    pb3 = pb.reshape(batch, seq, -1)
    out = pl.pallas_call(
        functools.partial(_hgrn_kernel, n_chunks=cs // HGRN_CHUNK, n_heads=hb,
                          part_heads=part_heads, n_seq=n_seq),
        grid=(batch // n_seq, n_groups, seq // cs),
        in_specs=[col(0), col(0), col(1), col(2), vec, vec],
        out_specs=col(0),
        out_shape=jax.ShapeDtypeStruct((batch, seq, d_model), BF16),
        scratch_shapes=[pltpu.VMEM((n_seq * hb, HGRN_HEAD, HGRN_HEAD), F32)],
        compiler_params=_cparams(("arbitrary", "arbitrary", "arbitrary")),
        name="hgrn2",
    )(pb3, f_pre.reshape(batch, seq, d_model), pb3, pb3,
      lb.reshape(1, d_model), a_norm.reshape(1, d_model))
    return out.reshape(m, d_model)


def _causal_conv(raw_ref, carry_ref, ext_ref, w_ref, b_ref, n_rows, n_taps, halo):
    first = halo - n_taps + 1
    outs = []
    for j in range(raw_ref.shape[1] // LANES):
        ln = slice(j * LANES, (j + 1) * LANES)
        raw = raw_ref[:, ln].astype(F32)
        ext_ref[j, 0:halo, :] = carry_ref[j]
        ext_ref[j, halo:halo + n_rows, :] = raw
        carry_ref[j] = raw[n_rows - halo:n_rows, :]
        acc = b_ref[:, ln] + w_ref[0:1, ln] * ext_ref[j, pl.ds(first, n_rows), :]
        for k in range(1, n_taps):
            acc = acc + w_ref[k:k + 1, ln] * ext_ref[j, pl.ds(first + k, n_rows), :]
        outs.append(acc)
    return jnp.concatenate(outs, axis=1)


def _ssd_kernel(x_ref, bm_ref, cm_ref, z_ref, u_ref, wdt_ref,
                cwx_ref, cwb_ref, cwc_ref, cbx_ref, cbb_ref, cbc_ref,
                dtb_ref, alog_ref, selp_ref, selw_ref, dskip_ref, norm_ref, o_ref,
                st_ref, carx_ref, carb_ref, carc_ref, extx_ref, extb_ref, extc_ref,
                *, heads_per_group, n_groups, n_heads):
    L, P, R, N = SSD_CHUNK, SSD_HEAD_DIM, heads_per_group, SSD_STATE
    gw = R * P
    groups = range(n_groups)
    g_lanes = [slice(j * gw, (j + 1) * gw) for j in groups]
    n_lanes = [slice(j * N, (j + 1) * N) for j in groups]

    @pl.when(pl.program_id(2) == 0)
    def _():
        st_ref[...] = jnp.zeros_like(st_ref)
        carx_ref[...] = jnp.zeros_like(carx_ref)
        carb_ref[...] = jnp.zeros_like(carb_ref)
        carc_ref[...] = jnp.zeros_like(carc_ref)

    halo = carx_ref.shape[1]
    xc = _silu(_causal_conv(x_ref, carx_ref, extx_ref, cwx_ref, cbx_ref, L, SSD_CONV, halo))
    bc16 = _silu(_causal_conv(bm_ref, carb_ref, extb_ref, cwb_ref, cbb_ref, L, SSD_CONV,
                              halo)).astype(BF16)
    cc16 = _silu(_causal_conv(cm_ref, carc_ref, extc_ref, cwc_ref, cbc_ref, L, SSD_CONV,
                              halo)).astype(BF16)

    lane = lax.broadcasted_iota(jnp.int32, (L, LANES), 1)
    dt_raw = _dot(u_ref[...], wdt_ref[...].astype(BF16))
    dt = jnp.where(lane < n_heads, _softplus(dt_raw + dtb_ref[...]), 0.0)
    dta = dt * (-LOG2_E * jnp.exp(alog_ref[...]))
    cs = _dot_exact_rhs(_lower_tri(L, BF16), dta)

    dt3 = _lane_stack3(dt, n_heads)
    cs3 = _lane_stack3(cs, n_heads)
    dt_x = [_dot(dt3, selp_ref[j]) for j in groups]
    cs_x = [_dot(cs3, selp_ref[j]) for j in groups]
    cs_w = [_dot(cs3, selw_ref[j]) for j in groups]

    rr = lax.broadcasted_iota(jnp.int32, (L, L), 0)
    cl = lax.broadcasted_iota(jnp.int32, (L, L), 1)
    causal = rr >= cl
    ys, xdt16s, cbs = [], [], []
    for j in groups:
        xdt = xc[:, g_lanes[j]] * dt_x[j]
        cs_last = cs_x[j][L - 1:L, :]
        st = st_ref[j]
        ys.append(_dot(cc16[:, n_lanes[j]], st.astype(BF16)) * jnp.exp2(cs_x[j]))
        st_ref[j] = st * jnp.exp2(cs_last) + _dot_tn(
            bc16[:, n_lanes[j]], (xdt * jnp.exp2(cs_last - cs_x[j])).astype(BF16))
        cb = _dot_nt(cc16[:, n_lanes[j]], bc16[:, n_lanes[j]])
        cbs.append(jnp.where(causal, cb, 0.0))
        xdt16s.append(xdt.astype(BF16))

    parts = [[] for _ in groups]
    for r in range(R):
        for j in groups:
            col = cs_w[j][:, r * L:(r + 1) * L]
            seg = col - col.T
            decay = jnp.exp2(jnp.minimum(seg, 0.0))
            parts[j].append(_dot((cbs[j] * decay).astype(BF16),
                                 xdt16s[j][:, r * P:(r + 1) * P]))

    for j in groups:
        ln = g_lanes[j]
        y = ys[j] + jnp.concatenate(parts[j], axis=1) + dskip_ref[:, ln] * xc[:, ln]
        yz = y * _silu(z_ref[:, ln].astype(F32))
        o_ref[:, ln] = _rms_rows(yz, norm_ref[:, ln]).astype(o_ref.dtype)


def ssd_mix(pb, u, w_dt, conv_w, conv_b, dt_bias, a_log, d_skip, b_norm, batch, seq, d_model):
    L, G, N = SSD_CHUNK, SSD_GROUPS, SSD_STATE
    n_heads = d_model // SSD_HEAD_DIM
    R = n_heads // G
    gw = d_model // G
    steps = seq // L
    m = batch * seq
    halo = 8
    gs = SSD_GROUPS_PER_STEP
    gsw, gsn = gs * gw, gs * N
    bn0 = 5 * d_model // gsn
    cwb0 = d_model // gsn

    def rep_heads(v):
        return jnp.repeat(v.astype(F32), SSD_HEAD_DIM).reshape(1, d_model)

    def pad_heads(v):
        return jnp.pad(v.astype(F32), (0, LANES - n_heads)).reshape(1, LANES)

    def select(width):
        row = jnp.arange(LANES, dtype=jnp.int32)[None, :, None]
        lane = jnp.arange(R * width, dtype=jnp.int32)[None, None, :]
        grp = jnp.arange(G, dtype=jnp.int32)[:, None, None]
        hit = (row % n_heads == grp * R + lane // width) & (row < 3 * n_heads)
        return hit.astype(BF16)

    row_g = lambda off: pl.BlockSpec((L, gsw), lambda b, g, s, off=off: (b * steps + s, off + g))
    row_n = lambda off: pl.BlockSpec((L, gsn), lambda b, g, s, off=off: (b * steps + s, off + g))
    par_g = lambda rows: pl.BlockSpec((rows, gsw), lambda b, g, s: (0, g))
    par_n = lambda rows, off: pl.BlockSpec((rows, gsn), lambda b, g, s, off=off: (0, off + g))
    head_vec = pl.BlockSpec((1, LANES), lambda b, g, s: (0, 0))
    sel = lambda width: pl.BlockSpec((gs, LANES, R * width), lambda b, g, s: (g, 0, 0))
    cb2 = conv_b.reshape(1, -1)
    n_steps_g = G // gs

    return pl.pallas_call(
        functools.partial(_ssd_kernel, heads_per_group=R, n_groups=gs, n_heads=n_heads),
        grid=(batch, n_steps_g, steps),
        in_specs=[row_g(4 * n_steps_g), row_n(bn0), row_n(bn0 + n_steps_g), row_g(3 * n_steps_g),
                  pl.BlockSpec((L, d_model), lambda b, g, s: (b * steps + s, 0)),
                  pl.BlockSpec((d_model, LANES), lambda b, g, s: (0, 0)),
                  par_g(SSD_CONV), par_n(SSD_CONV, cwb0), par_n(SSD_CONV, cwb0 + n_steps_g),
                  par_g(1), par_n(1, cwb0), par_n(1, cwb0 + n_steps_g),
                  head_vec, head_vec, sel(SSD_HEAD_DIM), sel(L), par_g(1), par_g(1)],
        out_specs=pl.BlockSpec((L, gsw), lambda b, g, s: (b * steps + s, g)),
        out_shape=jax.ShapeDtypeStruct((m, d_model), BF16),
        scratch_shapes=[pltpu.VMEM((gs, N, gw), F32),
                        pltpu.VMEM((gsw // LANES, halo, LANES), F32),
                        pltpu.VMEM((gsn // LANES, halo, LANES), F32),
                        pltpu.VMEM((gsn // LANES, halo, LANES), F32),
                        pltpu.VMEM((gsw // LANES, halo + L, LANES), F32),
                        pltpu.VMEM((gsn // LANES, halo + L, LANES), F32),
                        pltpu.VMEM((gsn // LANES, halo + L, LANES), F32)],
        compiler_params=_cparams(("arbitrary", "arbitrary", "arbitrary")),
        name="ssd",
    )(pb, pb, pb, pb, u, jnp.pad(w_dt, ((0, 0), (0, LANES - n_heads))),
      conv_w, conv_w, conv_w, cb2, cb2, cb2,
      pad_heads(dt_bias), pad_heads(a_log),
      select(SSD_HEAD_DIM), select(L), rep_heads(d_skip), b_norm.reshape(1, d_model))


def _conf_tail_kernel(c_ref, dww_ref, dwb_ref, lng_ref, lnb_ref, w2_hbm, b2_ref,
                      h_ref, gpost_ref, gnext_ref, hout_ref, unext_ref,
                      carry_ref, ext_ref, conv_ref, w2_ref, stage_ref, sem_ref,
                      *, row_block, layer):
    tm, d = c_ref.shape
    halo = CONF_HALO
    first = halo - (CONF_KERNEL - 1)
    n_tiles = d // LANES

    @pl.when((pl.program_id(0) == 0) & (pl.program_id(1) == 0))
    def _():
        _load_weight_bf16(w2_hbm, layer, w2_ref, stage_ref, sem_ref)

    @pl.when(pl.program_id(1) == 0)
    def _():
        carry_ref[...] = jnp.zeros_like(carry_ref)

    for j in range(n_tiles):
        ln = slice(j * LANES, (j + 1) * LANES)
        ext_ref[j, 0:halo, :] = carry_ref[j]
        ext_ref[j, halo:halo + tm, :] = c_ref[:, ln]
        carry_ref[j] = c_ref[tm - halo:tm, ln]

    def col_tile(j, carry):
        cj = pl.ds(pl.multiple_of(j * LANES, LANES), LANES)
        for rb in range(tm // row_block):
            base = rb * row_block
            acc = dwb_ref[:, cj] + dww_ref[0:1, cj] * ext_ref[j, pl.ds(base + first, row_block), :]
            for k in range(1, CONF_KERNEL):
                acc = acc + dww_ref[k:k + 1, cj] * ext_ref[j, pl.ds(base + first + k, row_block), :]
            conv_ref[base:base + row_block, cj] = acc
        return carry

    lax.fori_loop(0, n_tiles, col_tile, 0)

    x = conv_ref[...]
    mu = jnp.mean(x, axis=-1, keepdims=True)
    xc = x - mu
    var = jnp.mean(xc * xc, axis=-1, keepdims=True)
    y = _silu(xc * lax.rsqrt(var + LN_EPS) * lng_ref[...] + lnb_ref[...])
    m = _dot(y.astype(BF16), w2_ref[...]) + b2_ref[...]
    _residual_epilogue(m, h_ref[...], gpost_ref[...], gnext_ref[...], hout_ref, unext_ref)


def conf_tail(c, dw_w, dw_b, ln_g, ln_b, w2_stack, layer, b2, h, gpost, gnext,
              batch, seq, tm=512, row_block=64, stage_rows=WEIGHT_STAGE_ROWS // 2):
    m, d = c.shape
    tm = min(tm, seq)
    row_block = min(row_block, tm)
    stage_rows = min(stage_rows, d)
    steps = seq // tm
    row = pl.BlockSpec((tm, d), lambda b, s: (b * steps + s, 0))
    vec = pl.BlockSpec((1, d), lambda b, s: (0, 0))
    dww = jnp.pad(dw_w, ((0, CONF_HALO - CONF_KERNEL), (0, 0)))
    return pl.pallas_call(
        functools.partial(_conf_tail_kernel, row_block=row_block, layer=layer),
        grid=(batch, steps),
        in_specs=[row, pl.BlockSpec((CONF_HALO, d), lambda b, s: (0, 0)), vec, vec, vec,
                  pl.BlockSpec(memory_space=pl.ANY),
                  vec, row, vec, vec],
        out_specs=[row, row],
        out_shape=[jax.ShapeDtypeStruct((m, d), F32), jax.ShapeDtypeStruct((m, d), BF16)],
        scratch_shapes=[pltpu.VMEM((d // LANES, CONF_HALO, LANES), F32),
                        pltpu.VMEM((d // LANES, CONF_HALO + tm, LANES), F32),
                        pltpu.VMEM((tm, d), F32),
                        pltpu.VMEM((d, d), BF16), pltpu.VMEM((2, stage_rows, d), F32),
                        pltpu.SemaphoreType.DMA((2,))],
        compiler_params=_cparams(("arbitrary", "arbitrary")),
        name="conf_tail",
    )(c, dww, dw_b.reshape(1, d), ln_g.reshape(1, d), ln_b.reshape(1, d), w2_stack,
      b2.reshape(1, d), h, gpost.reshape(1, d), gnext.reshape(1, d))


def kernel(x, mix_pre_g, mix_post_g, ffn_pre_g, ffn_post_g, hgrn_lb_logits, even_w_in,
           hgrn_norm_g, ssd_conv_w, ssd_conv_b, ssd_dt_bias, ssd_a_log, ssd_d, ssd_norm_g,
           even_w_out, conf_w1, conf_b1, conf_dw_w, conf_dw_b, conf_ln_g, conf_ln_b,
           conf_w2, conf_b2, ffn_w_gate, ffn_w_up, ffn_w_down):
    batch, seq, d = x.shape
    depth = mix_pre_g.shape[0]
    hidden = ffn_w_gate.shape[2]
    m = batch * seq
    tn = 512
    tn_in = min(1024, d)
    main_cols = 6 * d + 2 * SSD_GROUPS * SSD_STATE
    n_ssd_heads = d // SSD_HEAD_DIM
    f_blocks = d // tn_in

    lb_p = jax.nn.softmax(hgrn_lb_logits.astype(F32), axis=0)
    lower_bounds = jnp.cumsum(lb_p, axis=0) - lb_p[0]

    w_in_t = jnp.swapaxes(even_w_in, 1, 2)
    conf_b1_3d = conf_b1.reshape(conf_b1.shape[0], 1, -1)
    plus = lambda off: (lambda n: n + off)

    h = x.reshape(m, d)
    u = prenorm(h, mix_pre_g[0])
    for layer in range(depth):
        i = layer // 2
        if layer % 2 == 0:
            skip_f = lambda n: n + jnp.where(n >= f_blocks, f_blocks, 0)
            pb = wide_proj(u, [w_in_t], i, [skip_f], [], [], (main_cols - d) // tn_in,
                           _identity_epilogue, BF16, "even_in_proj", tn=tn_in,
                           w_is_transposed=True)
            f_pre = wide_proj(u, [w_in_t], i, [plus(f_blocks)], [], [], f_blocks,
                              _identity_epilogue, F32, "even_f_proj", tn=tn_in,
                              w_is_transposed=True)
            o_a = hgrn_mix(pb, f_pre, lower_bounds[i], hgrn_norm_g[i], batch, seq, d)
            o_b = ssd_mix(pb, u, even_w_in[i, :, main_cols:], ssd_conv_w[i], ssd_conv_b[i],
                          ssd_dt_bias[i], ssd_a_log[i], ssd_d[i], ssd_norm_g[i], batch, seq, d)
            h, u = out_proj([o_a, o_b], even_w_out, i, h, mix_post_g[layer],
                            ffn_pre_g[layer], True, "even_out_proj")
        else:
            c = wide_proj(u, [conf_w1, conf_w1], i, [plus(0), plus(d // tn)],
                          [conf_b1_3d, conf_b1_3d], [plus(0), plus(d // tn)], d // tn,
                          _glu_epilogue, F32, "conf_glu", tn=tn)
            h, u = conf_tail(c, conf_dw_w[i], conf_dw_b[i], conf_ln_g[i], conf_ln_b[i],
                             conf_w2, i, conf_b2[i], h, mix_post_g[layer],
                             ffn_pre_g[layer], batch, seq)
        act = wide_proj(u, [ffn_w_gate, ffn_w_up], layer, [plus(0), plus(0)], [], [],
                        hidden // tn, _swiglu_epilogue, BF16, "ffn_in", tn=tn)
        last = layer == depth - 1
        gnext = mix_pre_g[layer] if last else mix_pre_g[layer + 1]
        h, u = out_proj([act], ffn_w_down, layer, h, ffn_post_g[layer], gnext, not last,
                        "ffn_out")
    return h.reshape(batch, seq, d)
```
